```python
import math
import jax
import jax.numpy as jnp
from jax import lax
import numpy as np

D_MODEL = 1024
BATCH = 8
SEQ = 4096
DEPTH = 2
DEC_BATCH = 8
DEC_SEQ = 2048
PAST_LEN = 128

GRID_W = 64
BLOCK = 128
HEAD_DIM = 64
MIX_WIDTH = D_MODEL
EPS = 1e-6
ROPE_THETA = 10000.0

A_HEADS = 4
A_KV = 2
WINDOW = 128
B_HEADS = 4
B_Q_RANK = 256
B_KV_RANK = 128
B_NOPE = 64
B_ROPE = 32
B_V = 64
C_HEADS = 4
C_KV = 2
D_HEADS = 4
D_QK = 32
D_V = 64

N_ALIBI = A_HEADS + D_HEADS

D_FF = 2816
CONV_W = 3

IN_SIZES = (
    A_HEADS * HEAD_DIM, A_KV * HEAD_DIM, A_KV * HEAD_DIM,
    B_Q_RANK, B_KV_RANK, B_ROPE,
    C_HEADS * HEAD_DIM, C_KV * HEAD_DIM, C_KV * HEAD_DIM,
    D_HEADS * 2 * D_QK, D_HEADS * 2 * D_QK, D_HEADS * D_V,
)
IN_WIDTH = sum(IN_SIZES)

kernel_name = "hybrid_parallel_heads_encoder"


def rms_norm(x, g):
    xf = x.astype(jnp.float32)
    y = xf * lax.rsqrt(jnp.mean(xf * xf, axis=-1, keepdims=True) + EPS)
    return (y * g.astype(jnp.float32)).astype(x.dtype)


def rope(x, pos):
    dim = x.shape[-1]
    half = dim // 2
    inv = ROPE_THETA ** (-jnp.arange(half, dtype=jnp.float32) * 2.0 / dim)
    ang = pos[:, None] * inv[None, :]
    cos = jnp.cos(ang)[:, None, :]
    sin = jnp.sin(ang)[:, None, :]
    xf = x.astype(jnp.float32)
    x1, x2 = xf[..., :half], xf[..., half:]
    return jnp.concatenate([x1 * cos - x2 * sin, x2 * cos + x1 * sin], axis=-1).astype(x.dtype)


def sweep_query_blocks(fn, qs):
    bsz, s_len = qs[0].shape[:2]
    nb = s_len // BLOCK
    blocks = tuple(jnp.swapaxes(a.reshape(bsz, nb, BLOCK, *a.shape[2:]), 0, 1) for a in qs)
    out = lax.map(fn, (jnp.arange(nb), *blocks))
    out = jnp.swapaxes(out, 0, 1)
    return out.reshape(bsz, s_len, *out.shape[3:])


def windowed_sink_attention(q, k, v, sink, slopes):
    bsz, s_len = q.shape[:2]
    nb = s_len // BLOCK
    grp = A_HEADS // A_KV
    qb = q.reshape(bsz, nb, BLOCK, A_KV, grp, HEAD_DIM)

    def band(t):
        tp = jnp.pad(t, ((0, 0), (BLOCK, BLOCK), (0, 0), (0, 0))).reshape(bsz, nb + 2, BLOCK, A_KV, HEAD_DIM)
        return jnp.concatenate([tp[:, :-2], tp[:, 1:-1], tp[:, 2:]], axis=2)

    kb, vb = band(k), band(v)
    s = jnp.einsum("bnqkgd,bnjkd->bnkgqj", qb, kb, preferred_element_type=jnp.float32) * (HEAD_DIM ** -0.5)
    blk = jnp.arange(nb)
    qpos = blk[:, None] * BLOCK + jnp.arange(BLOCK)[None, :]
    kpos = (blk[:, None] - 1) * BLOCK + jnp.arange(3 * BLOCK)[None, :]
    dist = jnp.abs(qpos[:, :, None] - kpos[:, None, :])
    valid = (dist <= WINDOW) & (kpos >= 0)[:, None, :] & (kpos < s_len)[:, None, :]
    m_h = slopes.reshape(A_KV, grp)[None, None, :, :, None, None]
    s = s - m_h * dist.astype(jnp.float32)[None, :, None, None]
    s = jnp.where(valid[None, :, None, None], s, -jnp.inf)
    sink_l = sink.astype(jnp.float32).reshape(A_KV, grp)[None, None, :, :, None, None]
    m = jnp.maximum(jnp.max(s, axis=-1, keepdims=True), sink_l)
    p = jnp.exp(s - m)
    p = p / (jnp.sum(p, axis=-1, keepdims=True) + jnp.exp(sink_l - m))
    o = jnp.einsum("bnkgqj,bnjkd->bnqkgd", p.astype(v.dtype), vb)
    return o.reshape(bsz, s_len, A_HEADS * HEAD_DIM)


def latent_attention(c_q, c_kv, k_rope_raw, q_norm, w_q_up, kv_norm, w_kv_up, pos):
    bsz, s_len = c_q.shape[:2]
    q = (rms_norm(c_q, q_norm) @ w_q_up).reshape(bsz, s_len, B_HEADS, B_NOPE + B_ROPE)
    q_nope = q[..., :B_NOPE]
    q_rope = rope(q[..., B_NOPE:], pos)
    kv = (rms_norm(c_kv, kv_norm) @ w_kv_up).reshape(bsz, s_len, B_HEADS, B_NOPE + B_V)
    k_nope, vh = kv[..., :B_NOPE], kv[..., B_NOPE:]
    k_rope = rope(k_rope_raw[:, :, None, :], pos)[:, :, 0]
    scale = (B_NOPE + B_ROPE) ** -0.5

    def block(args):
        _, qn, qr = args
        s = (jnp.einsum("bqhd,bshd->bhqs", qn, k_nope, preferred_element_type=jnp.float32)
             + jnp.einsum("bqhd,bsd->bhqs", qr, k_rope, preferred_element_type=jnp.float32)) * scale
        p = jax.nn.softmax(s, axis=-1).astype(vh.dtype)
        return jnp.einsum("bhqs,bshd->bqhd", p, vh)

    o = sweep_query_blocks(block, (q_nope, q_rope))
    return o.reshape(bsz, s_len, B_HEADS * B_V)


def axial_rope_gqa(q, k, v, q_norm, k_norm, row_pos, col_pos):
    bsz, s_len = q.shape[:2]
    grp = C_HEADS // C_KV
    half = HEAD_DIM // 2

    def prep(t, n_h, g):
        t = rms_norm(t.reshape(bsz, s_len, n_h, HEAD_DIM), g)
        return jnp.concatenate([rope(t[..., :half], row_pos), rope(t[..., half:], col_pos)], axis=-1)

    qh = prep(q, C_HEADS, q_norm).reshape(bsz, s_len, C_KV, grp, HEAD_DIM)
    kh = prep(k, C_KV, k_norm)
    vh = v.reshape(bsz, s_len, C_KV, HEAD_DIM)
    scale = HEAD_DIM ** -0.5

    def block(args):
        _, qb = args
        s = jnp.einsum("bqkgd,bskd->bkgqs", qb, kh, preferred_element_type=jnp.float32) * scale
        p = jax.nn.softmax(s, axis=-1).astype(vh.dtype)
        return jnp.einsum("bkgqs,bskd->bqkgd", p, vh)

    o = sweep_query_blocks(block, (qh,))
    return o.reshape(bsz, s_len, C_HEADS * HEAD_DIM)


def differential_attention(q, k, v, lq1, lk1, lq2, lk2, sub_norm, slopes, lam_init):
    bsz, s_len = q.shape[:2]
    f32 = jnp.float32
    qh = q.reshape(bsz, s_len, D_HEADS, 2, D_QK)
    kh = k.reshape(bsz, s_len, D_HEADS, 2, D_QK)
    vh = v.reshape(bsz, s_len, D_HEADS, D_V)
    lam = (jnp.exp(jnp.sum(lq1.astype(f32) * lk1.astype(f32)))
           - jnp.exp(jnp.sum(lq2.astype(f32) * lk2.astype(f32))) + lam_init)
    kpos = jnp.arange(s_len)
    scale = D_QK ** -0.5

    def block(args):
        i, qb = args
        s = jnp.einsum("bqhcd,bshcd->bhcqs", qb, kh, preferred_element_type=f32) * scale
        qpos = i * BLOCK + jnp.arange(BLOCK)
        dist = jnp.abs(qpos[:, None] - kpos[None, :]).astype(f32)
        s = s - slopes[None, :, None, None, None] * dist
        p = jax.nn.softmax(s, axis=-1)
        a = p[:, :, 0] - lam * p[:, :, 1]
        return jnp.einsum("bhqs,bshd->bqhd", a.astype(vh.dtype), vh)

    o = sweep_query_blocks(block, (qh,))
    o = rms_norm(o, sub_norm) * (1.0 - lam_init)
    return o.reshape(bsz, s_len, D_HEADS * D_V)


def conv_gated_mlp(x, w_up, b_up, conv_w, conv_b, w_down):
    h = x @ w_up + b_up
    hp = jnp.pad(h, ((0, 0), (CONV_W // 2, CONV_W // 2), (0, 0)))
    h = conv_w[0] * hp[:, :-2] + conv_w[1] * hp[:, 1:-1] + conv_w[2] * hp[:, 2:] + conv_b
    a, b = jnp.split(h, 2, axis=-1)
    return (jax.nn.silu(a) * b) @ w_down


def trunk(x, g_attn, w_in, a_sink, b_q_norm, b_w_q_up, b_kv_norm, b_w_kv_up, c_q_norm, c_k_norm,
          d_lambda_q1, d_lambda_k1, d_lambda_q2, d_lambda_k2, d_sub_norm, w_out,
          g_ffn, w_up, b_up, conv_w, conv_b, w_down, g_final):
    bsz, s_len, _ = x.shape
    f32 = jnp.float32
    pos = jnp.arange(s_len, dtype=f32)
    rows = s_len // GRID_W
    row_pos = jnp.broadcast_to(jnp.arange(rows, dtype=f32)[:, None], (rows, GRID_W)).reshape(s_len)
    col_pos = jnp.broadcast_to(jnp.arange(GRID_W, dtype=f32)[None, :], (rows, GRID_W)).reshape(s_len)
    slopes = 2.0 ** (-8.0 * (jnp.arange(N_ALIBI, dtype=f32) + 1.0) / N_ALIBI)
    split_at = np.cumsum(IN_SIZES)[:-1].tolist()
    for l in range(DEPTH):
        h = rms_norm(x, g_attn[l]) @ w_in[l]
        aq, ak, av, bcq, bckv, bkr, cq, ck, cv, dq, dk, dv = jnp.split(h, split_at, axis=-1)
        o_a = windowed_sink_attention(
            aq.reshape(bsz, s_len, A_HEADS, HEAD_DIM),
            ak.reshape(bsz, s_len, A_KV, HEAD_DIM),
            av.reshape(bsz, s_len, A_KV, HEAD_DIM),
            a_sink[l], slopes[:A_HEADS])
        o_b = latent_attention(bcq, bckv, bkr, b_q_norm[l], b_w_q_up[l], b_kv_norm[l], b_w_kv_up[l], pos)
        o_c = axial_rope_gqa(cq, ck, cv, c_q_norm[l], c_k_norm[l], row_pos, col_pos)
        lam_init = 0.8 - 0.6 * math.exp(-0.3 * l)
        o_d = differential_attention(dq, dk, dv, d_lambda_q1[l], d_lambda_k1[l], d_lambda_q2[l], d_lambda_k2[l],
                                     d_sub_norm[l], slopes[A_HEADS:], lam_init)
        x = x + jnp.concatenate([o_a, o_b, o_c, o_d], axis=-1) @ w_out[l]
        x = x + conv_gated_mlp(rms_norm(x, g_ffn[l]), w_up[l], b_up[l], conv_w[l], conv_b[l], w_down[l])
    return rms_norm(x, g_final)


def setup_inputs(seed: int = 0) -> dict:
    key = jax.random.key(seed)
    ks = jax.random.split(key, 24)
    f32 = jnp.float32
    L = DEPTH

    def nrm(k, shape, scale):
        return jax.random.normal(k, shape, f32) * scale

    def gain(k, shape):
        return 1.0 + 0.05 * jax.random.normal(k, shape, f32)

    return {
        "x_prompt": nrm(ks[0], (BATCH, SEQ, D_MODEL), 1.0),
        "x_sample": nrm(ks[1], (DEC_BATCH, DEC_SEQ, D_MODEL), 1.0),
        "g_attn": gain(ks[2], (L, D_MODEL)),
        "w_in": nrm(ks[3], (L, D_MODEL, IN_WIDTH), D_MODEL ** -0.5),
        "a_sink": nrm(ks[4], (L, A_HEADS), 0.5),
        "b_q_norm": gain(ks[5], (L, B_Q_RANK)),
        "b_w_q_up": nrm(ks[6], (L, B_Q_RANK, B_HEADS * (B_NOPE + B_ROPE)), B_Q_RANK ** -0.5),
        "b_kv_norm": gain(ks[7], (L, B_KV_RANK)),
        "b_w_kv_up": nrm(ks[8], (L, B_KV_RANK, B_HEADS * (B_NOPE + B_V)), B_KV_RANK ** -0.5),
        "c_q_norm": gain(ks[9], (L, HEAD_DIM)),
        "c_k_norm": gain(ks[10], (L, HEAD_DIM)),
        "d_lambda_q1": nrm(ks[11], (L, D_QK), 0.1),
        "d_lambda_k1": nrm(ks[12], (L, D_QK), 0.1),
        "d_lambda_q2": nrm(ks[13], (L, D_QK), 0.1),
        "d_lambda_k2": nrm(ks[14], (L, D_QK), 0.1),
        "d_sub_norm": gain(ks[15], (L, D_V)),
        "w_out": nrm(ks[16], (L, MIX_WIDTH, D_MODEL), MIX_WIDTH ** -0.5),
        "g_ffn": gain(ks[17], (L, D_MODEL)),
        "w_up": nrm(ks[18], (L, D_MODEL, 2 * D_FF), D_MODEL ** -0.5),
        "b_up": nrm(ks[19], (L, 2 * D_FF), 0.02),
        "conv_w": nrm(ks[20], (L, CONV_W, 2 * D_FF), CONV_W ** -0.5),
        "conv_b": nrm(ks[21], (L, 2 * D_FF), 0.02),
        "w_down": nrm(ks[22], (L, D_FF, D_MODEL), D_FF ** -0.5),
        "g_final": gain(ks[23], (D_MODEL,)),
    }


def reference(x_prompt, x_sample, g_attn, w_in, a_sink, b_q_norm, b_w_q_up, b_kv_norm, b_w_kv_up,
              c_q_norm, c_k_norm, d_lambda_q1, d_lambda_k1, d_lambda_q2, d_lambda_k2, d_sub_norm, w_out,
              g_ffn, w_up, b_up, conv_w, conv_b, w_down, g_final):
    y_prompt = trunk(x_prompt, g_attn, w_in, a_sink, b_q_norm, b_w_q_up, b_kv_norm, b_w_kv_up,
                     c_q_norm, c_k_norm, d_lambda_q1, d_lambda_k1, d_lambda_q2, d_lambda_k2, d_sub_norm, w_out,
                     g_ffn, w_up, b_up, conv_w, conv_b, w_down, g_final)
    y_sample = trunk(x_sample, g_attn, w_in, a_sink, b_q_norm, b_w_q_up, b_kv_norm, b_w_kv_up,
                     c_q_norm, c_k_norm, d_lambda_q1, d_lambda_k1, d_lambda_q2, d_lambda_k2, d_sub_norm, w_out,
                     g_ffn, w_up, b_up, conv_w, conv_b, w_down, g_final)
    return (y_prompt, y_sample)
```

```python
import functools
import math

import jax
import jax.numpy as jnp
import numpy as np
from jax import lax
from jax.experimental import pallas as pl
from jax.experimental.pallas import tpu as pltpu

F32 = jnp.float32
BF16 = jnp.bfloat16

D_MODEL = 1024
GRID_W = 64
BLOCK = 128
HEAD_DIM = 64
EPS = 1e-6
ROPE_THETA = 10000.0

A_HEADS = 4
WINDOW = 128
B_HEADS = 4
B_Q_RANK = 256
B_KV_RANK = 128
B_NOPE = 64
B_ROPE = 32
B_V = 64
D_QK = 32
D_V = 64
N_ALIBI = 8
D_FF = 2816
IN_SIZES = (256, 128, 128, 256, 128, 32, 256, 128, 128, 256, 256, 256)

SLOPES = tuple(2.0 ** (-8.0 * (i + 1.0) / N_ALIBI) for i in range(N_ALIBI))

LANES = 128
TM = 512
TK = 512
TQ = 256
TQA = 512
HALO = 16
FF_CHUNK = 256
N_FF_CHUNKS = D_FF // FF_CHUNK
VMEM_LIMIT = 56 * 1024 * 1024

_OFF_AQ, _OFF_AK, _OFF_AV = 0, 256, 384
_OFF_BCQ, _OFF_BCKV, _OFF_BKR = 512, 768, 896
_OFF_CQ, _OFF_CK, _OFF_CV = 1024, 1280, 1408
_OFF_DQ, _OFF_DK, _OFF_DV = 1536, 1792, 2048
IN_PACKED = 2304


def _dot(a, b):
    return jnp.dot(a, b, preferred_element_type=F32)


def _dot_nt(a, b):
    return lax.dot_general(a, b, (((1,), (1,)), ((), ())), preferred_element_type=F32)


def _rms(x, g):
    return x * lax.rsqrt(jnp.mean(x * x, axis=-1, keepdims=True) + EPS) * g


def _rope(t, tab_ref):
    return (t * tab_ref[0] + pltpu.roll(t, LANES - 16, 1) * tab_ref[1]
            + pltpu.roll(t, 16, 1) * tab_ref[2])


def _dup_halves(t):
    lane = lax.broadcasted_iota(jnp.int32, t.shape, 1)
    sw = pltpu.roll(t, 64, 1)
    lo = lane < 64
    return jnp.where(lo, t, sw), jnp.where(lo, sw, t)


def _head_mean_sq(t):
    w = t.shape[1]
    r = lax.broadcasted_iota(jnp.int32, (w, w), 0) // HEAD_DIM
    c = lax.broadcasted_iota(jnp.int32, (w, w), 1) // HEAD_DIM
    pool = jnp.where(r == c, 1.0 / HEAD_DIM, 0.0).astype(BF16)
    t2 = t * t
    hi = t2.astype(BF16)
    lo = (t2 - hi.astype(F32)).astype(BF16)
    return _dot(hi, pool) + _dot(lo, pool)


def _proj_kernel(x_ref, g_ref, win_ref, bqn_ref, wq_ref, bkvn_ref, wkv_ref, cqn_ref, ckn_ref,
                 ropeb_ref, ropec_ref,
                 qa_ref, ka_ref, va_ref, qb_ref, ktb_ref, vb_ref, qc_ref, ktc_ref, vc_ref,
                 qd_ref, ktd_ref, vd_ref):
    x = x_ref[0]
    xn = _rms(x, g_ref[...]).astype(BF16)
    h = _dot(xn, win_ref[...])

    qa_ref[0] = (h[:, _OFF_AQ:_OFF_AQ + 256] * (HEAD_DIM ** -0.5)).astype(BF16)
    k0, k1 = _dup_halves(h[:, _OFF_AK:_OFF_AK + 128])
    ka_ref[0, 0] = k0.astype(BF16)
    ka_ref[0, 1] = k1.astype(BF16)
    v0, v1 = _dup_halves(h[:, _OFF_AV:_OFF_AV + 128])
    va_ref[0, 0] = v0.astype(BF16)
    va_ref[0, 1] = v1.astype(BF16)

    cq = _rms(h[:, _OFF_BCQ:_OFF_BCQ + B_Q_RANK], bqn_ref[...]).astype(BF16)
    qb = _dot(cq, wq_ref[...])
    scale_b = (B_NOPE + B_ROPE) ** -0.5
    for hd in range(B_HEADS):
        t = _rope(qb[:, hd * 128:(hd + 1) * 128], ropeb_ref)
        qb_ref[0, :, hd * 128:(hd + 1) * 128] = (t * scale_b).astype(BF16)
    ckv = _rms(h[:, _OFF_BCKV:_OFF_BCKV + B_KV_RANK], bkvn_ref[...]).astype(BF16)
    kv = _dot(ckv, wkv_ref[...])
    kr = _rope(h[:, _OFF_BKR:_OFF_BKR + 128], ropeb_ref)
    for hd in range(B_HEADS):
        ktb_ref[0, hd, 0] = (kv[:, hd * 128:(hd + 1) * 128] + kr).T.astype(BF16)
    vb_ref[0] = kv[:, 512:768].astype(BF16)

    cqh = h[:, _OFF_CQ:_OFF_CQ + 256]
    cqh = cqh * lax.rsqrt(_head_mean_sq(cqh) + EPS) * cqn_ref[...]
    for grp in range(2):
        t = _rope(cqh[:, grp * 128:(grp + 1) * 128], ropec_ref)
        qc_ref[0, :, grp * 128:(grp + 1) * 128] = (t * (HEAD_DIM ** -0.5)).astype(BF16)
    ckh = h[:, _OFF_CK:_OFF_CK + 128]
    ckh = ckh * lax.rsqrt(_head_mean_sq(ckh) + EPS) * ckn_ref[...]
    ckh = _rope(ckh, ropec_ref)
    k0, k1 = _dup_halves(ckh)
    ktc_ref[0, 0, 0] = k0.T.astype(BF16)
    ktc_ref[0, 1, 0] = k1.T.astype(BF16)
    v0, v1 = _dup_halves(h[:, _OFF_CV:_OFF_CV + 128])
    vc_ref[0, 0] = v0.astype(BF16)
    vc_ref[0, 1] = v1.astype(BF16)

    qd_ref[0] = (h[:, _OFF_DQ:_OFF_DQ + 256] * (D_QK ** -0.5)).astype(BF16)
    for grp in range(2):
        ktd_ref[0, grp, 0] = h[:, _OFF_DK + grp * 128:_OFF_DK + (grp + 1) * 128].T.astype(BF16)
    vd_ref[0] = h[:, _OFF_DV:_OFF_DV + 256].astype(BF16)


def _const_spec(shape):
    nd = len(shape)
    return pl.BlockSpec(shape, lambda *_: (0,) * nd)


def _proj_call(x, lw, rope_b, rope_c):
    bsz, s_len, _ = x.shape
    nt = s_len // TM
    tok = lambda w: pl.BlockSpec((1, TM, w), lambda b, i: (b, i, 0))
    dup = pl.BlockSpec((1, 2, TM, 128), lambda b, i: (b, 0, i, 0))
    kt = lambda n: pl.BlockSpec((1, n, 1, 128, TK), lambda b, i: (b, 0, i, 0, 0))
    sds = jax.ShapeDtypeStruct
    out_shape = (
        sds((bsz, s_len, 256), BF16), sds((bsz, 2, s_len, 128), BF16), sds((bsz, 2, s_len, 128), BF16),
        sds((bsz, s_len, 512), BF16), sds((bsz, 4, nt, 128, TK), BF16), sds((bsz, s_len, 256), BF16),
        sds((bsz, s_len, 256), BF16), sds((bsz, 2, nt, 128, TK), BF16), sds((bsz, 2, s_len, 128), BF16),
        sds((bsz, s_len, 256), BF16), sds((bsz, 2, nt, 128, TK), BF16), sds((bsz, s_len, 256), BF16),
    )
    out_specs = (tok(256), dup, dup, tok(512), kt(4), tok(256),
                 tok(256), kt(2), dup, tok(256), kt(2), tok(256))
    rope_spec = pl.BlockSpec((3, TM, 128), lambda b, i: (0, i, 0))
    in_specs = [
        tok(D_MODEL), _const_spec((1, D_MODEL)), _const_spec((D_MODEL, IN_PACKED)),
        _const_spec((1, B_Q_RANK)), _const_spec((B_Q_RANK, 512)),
        _const_spec((1, B_KV_RANK)), _const_spec((B_KV_RANK, 768)),
        _const_spec((1, 256)), _const_spec((1, 128)),
        rope_spec, rope_spec,
    ]
    return pl.pallas_call(
        _proj_kernel,
        grid=(bsz, nt),
        in_specs=in_specs,
        out_specs=out_specs,
        out_shape=out_shape,
        compiler_params=pltpu.CompilerParams(
            dimension_semantics=("arbitrary", "arbitrary"), vmem_limit_bytes=VMEM_LIMIT),
        name="in_proj",
    )(x, lw["g_attn"], lw["w_in"], lw["b_q_norm"], lw["w_q_up"], lw["b_kv_norm"], lw["w_kv_up"],
      lw["c_q_norm"], lw["c_k_norm"], rope_b, rope_c)


def _fold(op, s):
    out = s[:, 0:LANES]
    for j in range(1, s.shape[1] // LANES):
        out = op(out, s[:, j * LANES:(j + 1) * LANES])
    return out


def _softmax_pv(qm, kt_at, v_at, n_chunks, s_scr, bias_at=None):
    rows = qm.shape[0]

    def scores(c, m_run):
        s = _dot(qm, kt_at(c))
        if bias_at is not None:
            s = s - bias_at(c)
        s_scr[c] = s
        return jnp.maximum(m_run, _fold(jnp.maximum, s))

    m_run = lax.fori_loop(0, n_chunks, scores, jnp.full((rows, LANES), -jnp.inf, F32))
    m = jnp.max(m_run, axis=-1, keepdims=True)

    def weigh(c, carry):
        l_run, acc = carry
        p = jnp.exp(s_scr[c] - m)
        return l_run + _fold(jnp.add, p), acc + _dot(p.astype(BF16), v_at(c))

    zeros = jnp.zeros((rows, LANES), F32)
    l_run, acc = lax.fori_loop(0, n_chunks, weigh, (zeros, zeros))
    return acc, jnp.sum(l_run, axis=-1, keepdims=True)


def _attn_b_kernel(q_ref, kt_ref, v_ref, o_ref, s_scr, *, n_chunks):
    lane = lax.broadcasted_iota(jnp.int32, (TQ, LANES), 1)
    outs = []
    for r in range(2):
        acc, l = _softmax_pv(q_ref[0, :, r * 128:(r + 1) * 128],
                             lambda c, r=r: kt_ref[0, r, c], lambda c: v_ref[0, c],
                             n_chunks, s_scr)
        outs.append(acc / l)
    o_ref[0] = jnp.where(lane < 64, outs[0], outs[1]).astype(BF16)


def _attn_c_kernel(q_ref, kt_ref, v_ref, o_ref, s_scr, *, n_chunks):
    lane = lax.broadcasted_iota(jnp.int32, (TQ, LANES), 1)
    q = q_ref[0]
    outs = []
    for r in range(2):
        qm = jnp.where((lane >= 64 * r) & (lane < 64 * (r + 1)), q, jnp.zeros_like(q))
        acc, l = _softmax_pv(qm, lambda c: kt_ref[0, 0, c], lambda c: v_ref[0, 0, c],
                             n_chunks, s_scr)
        outs.append(acc / l)
    o_ref[0] = jnp.where(lane < 64, outs[0], outs[1]).astype(BF16)


def _attn_d_kernel(q_ref, kt_ref, v_ref, lq1_ref, lk1_ref, lq2_ref, lk2_ref, sn_ref, o_ref, s_scr,
                   *, n_chunks, lam_init):
    grp = pl.program_id(1)
    q0 = pl.program_id(2) * TQ
    lam = (jnp.exp(jnp.sum(lq1_ref[...] * lk1_ref[...], axis=-1, keepdims=True))
           - jnp.exp(jnp.sum(lq2_ref[...] * lk2_ref[...], axis=-1, keepdims=True)) + lam_init)
    lane = lax.broadcasted_iota(jnp.int32, (TQ, LANES), 1)
    q = q_ref[0]
    rel = (lax.broadcasted_iota(jnp.int32, (TQ, TK), 0)
           - lax.broadcasted_iota(jnp.int32, (TQ, TK), 1)).astype(F32)
    outs = []
    for r in range(2):
        slope = jnp.where(grp == 0, SLOPES[A_HEADS + r], SLOPES[A_HEADS + 2 + r]).astype(F32)

        def bias_at(c, slope=slope):
            return slope * jnp.abs(rel + (q0 - c * TK).astype(F32))

        parts = []
        for cm in range(2):
            lo = 64 * r + 32 * cm
            qm = jnp.where((lane >= lo) & (lane < lo + D_QK), q, jnp.zeros_like(q))
            acc, l = _softmax_pv(qm, lambda c: kt_ref[0, 0, c], lambda c: v_ref[0, c],
                                 n_chunks, s_scr, bias_at)
            parts.append(acc / l)
        o = parts[0] - lam * parts[1]
        in_head = (lane >= 64 * r) & (lane < 64 * (r + 1))
        msq = jnp.sum(jnp.where(in_head, o * o, 0.0), axis=-1, keepdims=True) * (1.0 / D_V)
        outs.append(o * lax.rsqrt(msq + EPS) * sn_ref[...] * (1.0 - lam_init))
    o_ref[0] = jnp.where(lane < 64, outs[0], outs[1]).astype(BF16)


def _full_attn_call(kind, q, kt, v, extra=(), lam_init=0.0):
    bsz, s_len = q.shape[0], q.shape[1]
    nq, nc = s_len // TQ, s_len // TK
    out_spec = pl.BlockSpec((1, TQ, 128), lambda b, g, i: (b, i, g))
    if kind == "b":
        body = functools.partial(_attn_b_kernel, n_chunks=nc)
        in_specs = [pl.BlockSpec((1, TQ, 256), lambda b, g, i: (b, i, g)),
                    pl.BlockSpec((1, 2, nc, 128, TK), lambda b, g, i: (b, g, 0, 0, 0)),
                    pl.BlockSpec((1, nc, TK, 128), lambda b, g, i: (b, 0, 0, g))]
    elif kind == "c":
        body = functools.partial(_attn_c_kernel, n_chunks=nc)
        in_specs = [pl.BlockSpec((1, TQ, 128), lambda b, g, i: (b, i, g)),
                    pl.BlockSpec((1, 1, nc, 128, TK), lambda b, g, i: (b, g, 0, 0, 0)),
                    pl.BlockSpec((1, 1, nc, TK, 128), lambda b, g, i: (b, g, 0, 0, 0))]
    else:
        body = functools.partial(_attn_d_kernel, n_chunks=nc, lam_init=lam_init)
        in_specs = [pl.BlockSpec((1, TQ, 128), lambda b, g, i: (b, i, g)),
                    pl.BlockSpec((1, 1, nc, 128, TK), lambda b, g, i: (b, g, 0, 0, 0)),
                    pl.BlockSpec((1, nc, TK, 128), lambda b, g, i: (b, 0, 0, g)),
                    _const_spec((1, D_QK)), _const_spec((1, D_QK)),
                    _const_spec((1, D_QK)), _const_spec((1, D_QK)),
                    _const_spec((1, 128))]
    return pl.pallas_call(
        body,
        grid=(bsz, 2, nq),
        in_specs=in_specs,
        out_specs=out_spec,
        out_shape=jax.ShapeDtypeStruct((bsz, s_len, 256), BF16),
        scratch_shapes=[pltpu.VMEM((nc, TQ, TK), F32)],
        compiler_params=pltpu.CompilerParams(
            dimension_semantics=("arbitrary", "arbitrary", "arbitrary"),
            vmem_limit_bytes=VMEM_LIMIT),
        name="attn_" + kind,
    )(q, kt, v, *extra)


def _attn_a_kernel(q_ref, k_ref, v_ref, sink_ref, o_ref, *, s_len):
    tile = pl.program_id(1)
    span = 3 * BLOCK
    lane = lax.broadcasted_iota(jnp.int32, (BLOCK, LANES), 1)
    sink = sink_ref[...]
    for j in range(TQA // BLOCK):
        blk = tile * (TQA // BLOCK) + j
        start = pl.multiple_of(jnp.clip((blk - 1) * BLOCK, 0, s_len - span), BLOCK)
        qpos = blk * BLOCK + lax.broadcasted_iota(jnp.int32, (BLOCK, span), 0)
        kpos = start + lax.broadcasted_iota(jnp.int32, (BLOCK, span), 1)
        dist = jnp.abs(qpos - kpos)
        valid = dist <= WINDOW
        distf = dist.astype(F32)
        for grp in range(2):
            kk = k_ref[0, grp, pl.ds(start, span), :]
            vv = v_ref[0, grp, pl.ds(start, span), :]
            qg = q_ref[0, j * BLOCK:(j + 1) * BLOCK, grp * 128:(grp + 1) * 128]
            outs = []
            for r in range(2):
                hd = 2 * grp + r
                qm = jnp.where((lane >= 64 * r) & (lane < 64 * (r + 1)), qg, jnp.zeros_like(qg))
                s = _dot_nt(qm, kk) - SLOPES[hd] * distf
                s = jnp.where(valid, s, -jnp.inf)
                sk = sink[:, hd:hd + 1]
                m = jnp.maximum(jnp.max(s, axis=-1, keepdims=True), sk)
                p = jnp.exp(s - m)
                den = jnp.sum(p, axis=-1, keepdims=True) + jnp.exp(sk - m)
                outs.append(_dot(p.astype(BF16), vv) / den)
            o_ref[0, j * BLOCK:(j + 1) * BLOCK, grp * 128:(grp + 1) * 128] = (
                jnp.where(lane < 64, outs[0], outs[1]).astype(BF16))


def _attn_a_call(q, k, v, sink):
    bsz, s_len = q.shape[0], q.shape[1]
    kv_spec = pl.BlockSpec((1, 2, s_len, 128), lambda b, i: (b, 0, 0, 0))
    return pl.pallas_call(
        functools.partial(_attn_a_kernel, s_len=s_len),
        grid=(bsz, s_len // TQA),
        in_specs=[pl.BlockSpec((1, TQA, 256), lambda b, i: (b, i, 0)), kv_spec, kv_spec,
                  _const_spec((1, A_HEADS))],
        out_specs=pl.BlockSpec((1, TQA, 256), lambda b, i: (b, i, 0)),
        out_shape=jax.ShapeDtypeStruct((bsz, s_len, 256), BF16),
        compiler_params=pltpu.CompilerParams(
            dimension_semantics=("arbitrary", "arbitrary"), vmem_limit_bytes=VMEM_LIMIT),
        name="attn_a",
    )(q, k, v, sink)


def _out_proj_kernel(x_ref, oa_ref, ob_ref, oc_ref, od_ref, wo_ref, g_ref, x1_ref, xn_ref):
    x1 = x_ref[0]
    for j, o_ref in enumerate((oa_ref, ob_ref, oc_ref, od_ref)):
        x1 = x1 + _dot(o_ref[0], wo_ref[j * 256:(j + 1) * 256, :])
    x1_ref[0] = x1
    xn_ref[0] = _rms(x1, g_ref[...]).astype(BF16)


def _out_proj_call(x, oa, ob, oc, od, lw):
    bsz, s_len, _ = x.shape
    tok = lambda w: pl.BlockSpec((1, TM, w), lambda b, i: (b, i, 0))
    return pl.pallas_call(
        _out_proj_kernel,
        grid=(bsz, s_len // TM),
        in_specs=[tok(D_MODEL), tok(256), tok(256), tok(256), tok(256),
                  _const_spec((D_MODEL, D_MODEL)), _const_spec((1, D_MODEL))],
        out_specs=(tok(D_MODEL), tok(D_MODEL)),
        out_shape=(jax.ShapeDtypeStruct((bsz, s_len, D_MODEL), F32),
                   jax.ShapeDtypeStruct((bsz, s_len, D_MODEL), BF16)),
        compiler_params=pltpu.CompilerParams(
            dimension_semantics=("arbitrary", "arbitrary"), vmem_limit_bytes=VMEM_LIMIT),
        name="out_proj",
    )(x, oa, ob, oc, od, lw["w_out"], lw["g_ffn"])


def _ffn_kernel(xn_ref, prev_ref, next_ref, x1_ref, wua_ref, wub_ref, vec_ref, wd_ref, gf_ref,
                o_ref, xext_ref, *, s_len, final_norm):
    t0 = pl.program_id(1) * TM
    xext_ref[0:HALO] = prev_ref[0]
    xext_ref[HALO:HALO + TM] = xn_ref[0]
    xext_ref[HALO + TM:HALO + TM + HALO] = next_ref[0]
    tpos = t0 - HALO + lax.broadcasted_iota(jnp.int32, (TM + 2 * HALO, 1), 0)
    valid = (tpos >= 0) & (tpos < s_len)

    def conv(hx, w0, w1, w2, b):
        return (w0 * hx[HALO - 1:HALO - 1 + TM] + w1 * hx[HALO:HALO + TM]
                + w2 * hx[HALO + 1:HALO + 1 + TM] + b)

    def chunk(c, acc):
        xe = xext_ref[...]
        vec = vec_ref[c]
        ha = jnp.where(valid, _dot(xe, wua_ref[c]) + vec[0:1], 0.0)
        hb = jnp.where(valid, _dot(xe, wub_ref[c]) + vec[1:2], 0.0)
        ca = conv(ha, vec[2:3], vec[3:4], vec[4:5], vec[8:9])
        cb = conv(hb, vec[5:6], vec[6:7], vec[7:8], vec[9:10])
        act = (ca / (1.0 + jnp.exp(-ca))) * cb
        return acc + _dot(act.astype(BF16), wd_ref[c])

    acc = lax.fori_loop(0, N_FF_CHUNKS, chunk, jnp.zeros((TM, D_MODEL), F32))
    out = x1_ref[0] + acc
    if final_norm:
        out = _rms(out, gf_ref[...])
    o_ref[0] = out


def _ffn_call(xn, x1, lw, g_final, final_norm):
    bsz, s_len, _ = x1.shape
    per_tile = TM // HALO
    last = s_len // HALO - 1
    tok = pl.BlockSpec((1, TM, D_MODEL), lambda b, i: (b, i, 0))
    prev = pl.BlockSpec((1, HALO, D_MODEL), lambda b, i: (b, jnp.maximum(i * per_tile - 1, 0), 0))
    nxt = pl.BlockSpec((1, HALO, D_MODEL),
                       lambda b, i: (b, jnp.minimum((i + 1) * per_tile, last), 0))
    return pl.pallas_call(
        functools.partial(_ffn_kernel, s_len=s_len, final_norm=final_norm),
        grid=(bsz, s_len // TM),
        in_specs=[tok, prev, nxt, tok,
                  _const_spec((N_FF_CHUNKS, D_MODEL, FF_CHUNK)),
                  _const_spec((N_FF_CHUNKS, D_MODEL, FF_CHUNK)),
                  _const_spec((N_FF_CHUNKS, 16, FF_CHUNK)),
                  _const_spec((N_FF_CHUNKS, FF_CHUNK, D_MODEL)),
                  _const_spec((1, D_MODEL))],
        out_specs=tok,
        out_shape=jax.ShapeDtypeStruct((bsz, s_len, D_MODEL), F32),
        scratch_shapes=[pltpu.VMEM((TM + 2 * HALO, D_MODEL), BF16)],
        compiler_params=pltpu.CompilerParams(
            dimension_semantics=("arbitrary", "arbitrary"), vmem_limit_bytes=VMEM_LIMIT),
        name="ffn",
    )(xn, xn, xn, x1, lw["w_up_a"], lw["w_up_b"], lw["ffn_vec"], lw["w_down"], g_final)


def _pack_layer(p, l):
    pieces = jnp.split(p["w_in"][l], np.cumsum(IN_SIZES)[:-1].tolist(), axis=1)
    aq, ak, av, bcq, bckv, bkr, cq, ck, cv, dq, dk, dv = pieces
    z = lambda n: jnp.zeros((D_MODEL, n), F32)
    w_in = jnp.concatenate([aq, ak, av, bcq, bckv, z(64), bkr, z(32), cq, ck, cv, dq, dk, dv], axis=1)

    wq = p["b_w_q_up"][l].reshape(B_Q_RANK, B_HEADS, B_NOPE + B_ROPE)
    wq = jnp.pad(wq, ((0, 0), (0, 0), (0, 128 - B_NOPE - B_ROPE))).reshape(B_Q_RANK, B_HEADS * 128)
    wkv = p["b_w_kv_up"][l].reshape(B_KV_RANK, B_HEADS, B_NOPE + B_V)
    wk = jnp.pad(wkv[:, :, :B_NOPE], ((0, 0), (0, 0), (0, 128 - B_NOPE))).reshape(B_KV_RANK, B_HEADS * 128)
    wv = wkv[:, :, B_NOPE:].reshape(B_KV_RANK, B_HEADS * B_V)

    w_up = p["w_up"][l]
    chunked = lambda w: w.reshape(D_MODEL, N_FF_CHUNKS, FF_CHUNK).transpose(1, 0, 2)
    halves = lambda v: (v[..., :D_FF], v[..., D_FF:])
    bua, bub = halves(p["b_up"][l])
    cwa, cwb = halves(p["conv_w"][l])
    cba, cbb = halves(p["conv_b"][l])
    rows = [bua, bub, cwa[0], cwa[1], cwa[2], cwb[0], cwb[1], cwb[2], cba, cbb]
    vec = jnp.stack(rows + [jnp.zeros_like(bua)] * (16 - len(rows)), axis=0)
    vec = vec.reshape(16, N_FF_CHUNKS, FF_CHUNK).transpose(1, 0, 2)
    row = lambda v: v.reshape(1, -1)
    return {
        "g_attn": row(p["g_attn"][l]),
        "w_in": w_in.astype(BF16),
        "a_sink": row(p["a_sink"][l]),
        "b_q_norm": row(p["b_q_norm"][l]),
        "w_q_up": wq.astype(BF16),
        "b_kv_norm": row(p["b_kv_norm"][l]),
        "w_kv_up": jnp.concatenate([wk, wv], axis=1).astype(BF16),
        "c_q_norm": row(jnp.tile(p["c_q_norm"][l], 4)),
        "c_k_norm": row(jnp.tile(p["c_k_norm"][l], 2)),
        "d_lambda": tuple(row(p[n][l]) for n in ("d_lambda_q1", "d_lambda_k1", "d_lambda_q2", "d_lambda_k2")),
        "d_sub_norm": row(jnp.tile(p["d_sub_norm"][l], 2)),
        "w_out": p["w_out"][l].astype(BF16),
        "g_ffn": row(p["g_ffn"][l]),
        "w_up_a": chunked(w_up[:, :D_FF]).astype(BF16),
        "w_up_b": chunked(w_up[:, D_FF:]).astype(BF16),
        "ffn_vec": vec,
        "w_down": p["w_down"][l].reshape(N_FF_CHUNKS, FF_CHUNK, D_MODEL).astype(BF16),
    }


def _rope_tables(s_len):
    half = 16
    inv = ROPE_THETA ** (-jnp.arange(half, dtype=F32) * 2.0 / (2 * half))
    pos = jnp.arange(s_len, dtype=F32)
    rows = s_len // GRID_W
    row_pos = jnp.broadcast_to(jnp.arange(rows, dtype=F32)[:, None], (rows, GRID_W)).reshape(s_len)
    col_pos = jnp.broadcast_to(jnp.arange(GRID_W, dtype=F32)[None, :], (rows, GRID_W)).reshape(s_len)

    def cs(p):
        ang = p[:, None] * inv[None, :]
        return jnp.cos(ang), jnp.sin(ang)

    one = lambda n: jnp.ones((s_len, n), F32)
    zero = lambda n: jnp.zeros((s_len, n), F32)
    c, s = cs(pos)
    rope_b = jnp.stack([
        jnp.concatenate([one(64), c, c, one(32)], axis=1),
        jnp.concatenate([zero(64), -s, zero(16), zero(32)], axis=1),
        jnp.concatenate([zero(64), zero(16), s, zero(32)], axis=1)])
    cr, sr = cs(row_pos)
    cc, sc = cs(col_pos)
    z = zero(16)
    rope_c = jnp.stack([
        jnp.tile(jnp.concatenate([cr, cr, cc, cc], axis=1), (1, 2)),
        jnp.tile(jnp.concatenate([-sr, z, -sc, z], axis=1), (1, 2)),
        jnp.tile(jnp.concatenate([z, sr, z, sc], axis=1), (1, 2))])
    return rope_b, rope_c


def _trunk(x, layers, g_final):
    bsz, s_len, _ = x.shape
    nc = s_len // TK
    rope_b, rope_c = _rope_tables(s_len)
    depth = len(layers)
    for l, lw in enumerate(layers):
        qa, ka, va, qb, ktb, vb, qc, ktc, vc, qd, ktd, vd = _proj_call(x, lw, rope_b, rope_c)
        oa = _attn_a_call(qa, ka, va, lw["a_sink"])
        ob = _full_attn_call("b", qb, ktb, vb.reshape(bsz, nc, TK, 256))
        oc = _full_attn_call("c", qc, ktc, vc.reshape(bsz, 2, nc, TK, 128))
        lam_init = 0.8 - 0.6 * math.exp(-0.3 * l)
        od = _full_attn_call("d", qd, ktd, vd.reshape(bsz, nc, TK, 256),
                             extra=(*lw["d_lambda"], lw["d_sub_norm"]), lam_init=lam_init)
        x1, xn = _out_proj_call(x, oa, ob, oc, od, lw)
        x = _ffn_call(xn, x1, lw, g_final, final_norm=(l == depth - 1))
    return x


def kernel(x_prompt, x_sample, g_attn, w_in, a_sink, b_q_norm, b_w_q_up, b_kv_norm, b_w_kv_up,
           c_q_norm, c_k_norm, d_lambda_q1, d_lambda_k1, d_lambda_q2, d_lambda_k2, d_sub_norm, w_out,
           g_ffn, w_up, b_up, conv_w, conv_b, w_down, g_final):
    p = dict(g_attn=g_attn, w_in=w_in, a_sink=a_sink, b_q_norm=b_q_norm, b_w_q_up=b_w_q_up,
             b_kv_norm=b_kv_norm, b_w_kv_up=b_w_kv_up, c_q_norm=c_q_norm, c_k_norm=c_k_norm,
             d_lambda_q1=d_lambda_q1, d_lambda_k1=d_lambda_k1, d_lambda_q2=d_lambda_q2,
             d_lambda_k2=d_lambda_k2, d_sub_norm=d_sub_norm, w_out=w_out, g_ffn=g_ffn, w_up=w_up,
             b_up=b_up, conv_w=conv_w, conv_b=conv_b, w_down=w_down)
    layers = [_pack_layer(p, l) for l in range(g_attn.shape[0])]
    gf = g_final.reshape(1, -1)
    return _trunk(x_prompt, layers, gf), _trunk(x_sample, layers, gf)
```

```python
import functools
import math

import jax
import jax.numpy as jnp
import numpy as np
from jax import lax
from jax.experimental import pallas as pl
from jax.experimental.pallas import tpu as pltpu

F32 = jnp.float32
BF16 = jnp.bfloat16

D_MODEL = 1024
GRID_W = 64
BLOCK = 128
HEAD_DIM = 64
EPS = 1e-6
ROPE_THETA = 10000.0

A_HEADS = 4
WINDOW = 128
B_HEADS = 4
B_Q_RANK = 256
B_KV_RANK = 128
B_NOPE = 64
B_ROPE = 32
B_V = 64
D_QK = 32
D_V = 64
N_ALIBI = 8
D_FF = 2816
IN_SIZES = (256, 128, 128, 256, 128, 32, 256, 128, 128, 256, 256, 256)

SLOPES = tuple(2.0 ** (-8.0 * (i + 1.0) / N_ALIBI) for i in range(N_ALIBI))

LANES = 128
TM = 512
TK = 512
TQ = 256
TQA = 512
HALO = 16
FF_CHUNK = 256
N_FF_CHUNKS = D_FF // FF_CHUNK
VMEM_LIMIT = 56 * 1024 * 1024

_OFF_AQ, _OFF_AK, _OFF_AV = 0, 256, 384
_OFF_BCQ, _OFF_BCKV, _OFF_BKR = 512, 768, 896
_OFF_CQ, _OFF_CK, _OFF_CV = 1024, 1280, 1408
_OFF_DQ, _OFF_DK, _OFF_DV = 1536, 1792, 2048
IN_PACKED = 2304


def _dot(a, b):
    return jnp.dot(a, b, preferred_element_type=F32)


def _dot_nt(a, b):
    return lax.dot_general(a, b, (((1,), (1,)), ((), ())), preferred_element_type=F32)


def _rms(x, g):
    return x * lax.rsqrt(jnp.mean(x * x, axis=-1, keepdims=True) + EPS) * g


def _rope(t, tab_ref):
    return (t * tab_ref[0] + pltpu.roll(t, LANES - 16, 1) * tab_ref[1]
            + pltpu.roll(t, 16, 1) * tab_ref[2])


def _dup_halves(t):
    lane = lax.broadcasted_iota(jnp.int32, t.shape, 1)
    sw = pltpu.roll(t, 64, 1)
    lo = lane < 64
    return jnp.where(lo, t, sw), jnp.where(lo, sw, t)


def _head_mean_sq(t):
    w = t.shape[1]
    r = lax.broadcasted_iota(jnp.int32, (w, w), 0) // HEAD_DIM
    c = lax.broadcasted_iota(jnp.int32, (w, w), 1) // HEAD_DIM
    pool = jnp.where(r == c, 1.0 / HEAD_DIM, 0.0).astype(BF16)
    t2 = t * t
    hi = t2.astype(BF16)
    lo = (t2 - hi.astype(F32)).astype(BF16)
    return _dot(hi, pool) + _dot(lo, pool)


def _proj_kernel(x_ref, g_ref, win_ref, bqn_ref, wq_ref, bkvn_ref, wkv_ref, cqn_ref, ckn_ref,
                 ropeb_ref, ropec_ref,
                 qa_ref, ka_ref, va_ref, qb_ref, ktb_ref, vb_ref, qc_ref, ktc_ref, vc_ref,
                 qd_ref, ktd_ref, vd_ref):
    x = x_ref[0]
    xn = _rms(x, g_ref[...]).astype(BF16)
    h = _dot(xn, win_ref[...])

    qa_ref[0] = (h[:, _OFF_AQ:_OFF_AQ + 256] * (HEAD_DIM ** -0.5)).astype(BF16)
    k0, k1 = _dup_halves(h[:, _OFF_AK:_OFF_AK + 128])
    ka_ref[0, 0] = k0.astype(BF16)
    ka_ref[0, 1] = k1.astype(BF16)
    v0, v1 = _dup_halves(h[:, _OFF_AV:_OFF_AV + 128])
    va_ref[0, 0] = v0.astype(BF16)
    va_ref[0, 1] = v1.astype(BF16)

    cq = _rms(h[:, _OFF_BCQ:_OFF_BCQ + B_Q_RANK], bqn_ref[...]).astype(BF16)
    qb = _dot(cq, wq_ref[...])
    scale_b = (B_NOPE + B_ROPE) ** -0.5
    for hd in range(B_HEADS):
        t = _rope(qb[:, hd * 128:(hd + 1) * 128], ropeb_ref)
        qb_ref[0, :, hd * 128:(hd + 1) * 128] = (t * scale_b).astype(BF16)
    ckv = _rms(h[:, _OFF_BCKV:_OFF_BCKV + B_KV_RANK], bkvn_ref[...]).astype(BF16)
    kv = _dot(ckv, wkv_ref[...])
    kr = _rope(h[:, _OFF_BKR:_OFF_BKR + 128], ropeb_ref)
    for hd in range(B_HEADS):
        ktb_ref[0, hd, 0] = (kv[:, hd * 128:(hd + 1) * 128] + kr).T.astype(BF16)
    vb_ref[0] = kv[:, 512:768].astype(BF16)

    cqh = h[:, _OFF_CQ:_OFF_CQ + 256]
    cqh = cqh * lax.rsqrt(_head_mean_sq(cqh) + EPS) * cqn_ref[...]
    for grp in range(2):
        t = _rope(cqh[:, grp * 128:(grp + 1) * 128], ropec_ref)
        qc_ref[0, :, grp * 128:(grp + 1) * 128] = (t * (HEAD_DIM ** -0.5)).astype(BF16)
    ckh = h[:, _OFF_CK:_OFF_CK + 128]
    ckh = ckh * lax.rsqrt(_head_mean_sq(ckh) + EPS) * ckn_ref[...]
    ckh = _rope(ckh, ropec_ref)
    k0, k1 = _dup_halves(ckh)
    ktc_ref[0, 0, 0] = k0.T.astype(BF16)
    ktc_ref[0, 1, 0] = k1.T.astype(BF16)
    v0, v1 = _dup_halves(h[:, _OFF_CV:_OFF_CV + 128])
    vc_ref[0, 0] = v0.astype(BF16)
    vc_ref[0, 1] = v1.astype(BF16)

    qd_ref[0] = (h[:, _OFF_DQ:_OFF_DQ + 256] * (D_QK ** -0.5)).astype(BF16)
    for grp in range(2):
        ktd_ref[0, grp, 0] = h[:, _OFF_DK + grp * 128:_OFF_DK + (grp + 1) * 128].T.astype(BF16)
    vd_ref[0] = h[:, _OFF_DV:_OFF_DV + 256].astype(BF16)


def _const_spec(shape):
    nd = len(shape)
    return pl.BlockSpec(shape, lambda *_: (0,) * nd)


def _proj_call(x, lw, rope_b, rope_c):
    bsz, s_len, _ = x.shape
    nt = s_len // TM
    tok = lambda w: pl.BlockSpec((1, TM, w), lambda b, i: (b, i, 0))
    dup = pl.BlockSpec((1, 2, TM, 128), lambda b, i: (b, 0, i, 0))
    kt = lambda n: pl.BlockSpec((1, n, 1, 128, TK), lambda b, i: (b, 0, i, 0, 0))
    sds = jax.ShapeDtypeStruct
    out_shape = (
        sds((bsz, s_len, 256), BF16), sds((bsz, 2, s_len, 128), BF16), sds((bsz, 2, s_len, 128), BF16),
        sds((bsz, s_len, 512), BF16), sds((bsz, 4, nt, 128, TK), BF16), sds((bsz, s_len, 256), BF16),
        sds((bsz, s_len, 256), BF16), sds((bsz, 2, nt, 128, TK), BF16), sds((bsz, 2, s_len, 128), BF16),
        sds((bsz, s_len, 256), BF16), sds((bsz, 2, nt, 128, TK), BF16), sds((bsz, s_len, 256), BF16),
    )
    out_specs = (tok(256), dup, dup, tok(512), kt(4), tok(256),
                 tok(256), kt(2), dup, tok(256), kt(2), tok(256))
    rope_spec = pl.BlockSpec((3, TM, 128), lambda b, i: (0, i, 0))
    in_specs = [
        tok(D_MODEL), _const_spec((1, D_MODEL)), _const_spec((D_MODEL, IN_PACKED)),
        _const_spec((1, B_Q_RANK)), _const_spec((B_Q_RANK, 512)),
        _const_spec((1, B_KV_RANK)), _const_spec((B_KV_RANK, 768)),
        _const_spec((1, 256)), _const_spec((1, 128)),
        rope_spec, rope_spec,
    ]
    return pl.pallas_call(
        _proj_kernel,
        grid=(bsz, nt),
        in_specs=in_specs,
        out_specs=out_specs,
        out_shape=out_shape,
        compiler_params=pltpu.CompilerParams(
            dimension_semantics=("arbitrary", "arbitrary"), vmem_limit_bytes=VMEM_LIMIT),
        name="in_proj",
    )(x, lw["g_attn"], lw["w_in"], lw["b_q_norm"], lw["w_q_up"], lw["b_kv_norm"], lw["w_kv_up"],
      lw["c_q_norm"], lw["c_k_norm"], rope_b, rope_c)


def _fold(op, s):
    out = s[:, 0:LANES]
    for j in range(1, s.shape[1] // LANES):
        out = op(out, s[:, j * LANES:(j + 1) * LANES])
    return out


def _softmax_pv(qm, kt_at, v_at, n_chunks, s_scr, bias_at=None):
    rows = qm.shape[0]

    def scores(c, m_run):
        s = _dot(qm, kt_at(c))
        if bias_at is not None:
            s = s - bias_at(c)
        s_scr[c] = s
        return jnp.maximum(m_run, _fold(jnp.maximum, s))

    m_run = lax.fori_loop(0, n_chunks, scores, jnp.full((rows, LANES), -jnp.inf, F32), unroll=True)
    m = jnp.max(m_run, axis=-1, keepdims=True)

    def weigh(c, carry):
        l_run, acc = carry
        p = jnp.exp(s_scr[c] - m)
        return l_run + _fold(jnp.add, p), acc + _dot(p.astype(BF16), v_at(c))

    zeros = jnp.zeros((rows, LANES), F32)
    l_run, acc = lax.fori_loop(0, n_chunks, weigh, (zeros, zeros), unroll=True)
    return acc, jnp.sum(l_run, axis=-1, keepdims=True)


def _attn_b_kernel(q_ref, kt_ref, v_ref, o_ref, s_scr, *, n_chunks):
    lane = lax.broadcasted_iota(jnp.int32, (TQ, LANES), 1)
    outs = []
    for r in range(2):
        acc, l = _softmax_pv(q_ref[0, :, r * 128:(r + 1) * 128],
                             lambda c, r=r: kt_ref[0, r, c], lambda c: v_ref[0, c],
                             n_chunks, s_scr)
        outs.append(acc / l)
    o_ref[0] = jnp.where(lane < 64, outs[0], outs[1]).astype(BF16)


def _attn_c_kernel(q_ref, kt_ref, v_ref, o_ref, s_scr, *, n_chunks):
    lane = lax.broadcasted_iota(jnp.int32, (TQ, LANES), 1)
    q = q_ref[0]
    outs = []
    for r in range(2):
        qm = jnp.where((lane >= 64 * r) & (lane < 64 * (r + 1)), q, jnp.zeros_like(q))
        acc, l = _softmax_pv(qm, lambda c: kt_ref[0, 0, c], lambda c: v_ref[0, 0, c],
                             n_chunks, s_scr)
        outs.append(acc / l)
    o_ref[0] = jnp.where(lane < 64, outs[0], outs[1]).astype(BF16)


def _attn_d_kernel(q_ref, kt_ref, v_ref, lq1_ref, lk1_ref, lq2_ref, lk2_ref, sn_ref, o_ref, s_scr,
                   *, n_chunks, lam_init):
    grp = pl.program_id(1)
    q0 = pl.program_id(2) * TQ
    lam = (jnp.exp(jnp.sum(lq1_ref[...] * lk1_ref[...], axis=-1, keepdims=True))
           - jnp.exp(jnp.sum(lq2_ref[...] * lk2_ref[...], axis=-1, keepdims=True)) + lam_init)
    lane = lax.broadcasted_iota(jnp.int32, (TQ, LANES), 1)
    q = q_ref[0]
    rel = (lax.broadcasted_iota(jnp.int32, (TQ, TK), 0)
           - lax.broadcasted_iota(jnp.int32, (TQ, TK), 1)).astype(F32)
    outs = []
    for r in range(2):
        slope = jnp.where(grp == 0, SLOPES[A_HEADS + r], SLOPES[A_HEADS + 2 + r]).astype(F32)

        def bias_at(c, slope=slope):
            return slope * jnp.abs(rel + (q0 - c * TK).astype(F32))

        parts = []
        for cm in range(2):
            lo = 64 * r + 32 * cm
            qm = jnp.where((lane >= lo) & (lane < lo + D_QK), q, jnp.zeros_like(q))
            acc, l = _softmax_pv(qm, lambda c: kt_ref[0, 0, c], lambda c: v_ref[0, c],
                                 n_chunks, s_scr, bias_at)
            parts.append(acc / l)
        o = parts[0] - lam * parts[1]
        in_head = (lane >= 64 * r) & (lane < 64 * (r + 1))
        msq = jnp.sum(jnp.where(in_head, o * o, 0.0), axis=-1, keepdims=True) * (1.0 / D_V)
        outs.append(o * lax.rsqrt(msq + EPS) * sn_ref[...] * (1.0 - lam_init))
    o_ref[0] = jnp.where(lane < 64, outs[0], outs[1]).astype(BF16)


def _full_attn_call(kind, q, kt, v, extra=(), lam_init=0.0):
    bsz, s_len = q.shape[0], q.shape[1]
    nq, nc = s_len // TQ, s_len // TK
    out_spec = pl.BlockSpec((1, TQ, 128), lambda b, g, i: (b, i, g))
    if kind == "b":
        body = functools.partial(_attn_b_kernel, n_chunks=nc)
        in_specs = [pl.BlockSpec((1, TQ, 256), lambda b, g, i: (b, i, g)),
                    pl.BlockSpec((1, 2, nc, 128, TK), lambda b, g, i: (b, g, 0, 0, 0)),
                    pl.BlockSpec((1, nc, TK, 128), lambda b, g, i: (b, 0, 0, g))]
    elif kind == "c":
        body = functools.partial(_attn_c_kernel, n_chunks=nc)
        in_specs = [pl.BlockSpec((1, TQ, 128), lambda b, g, i: (b, i, g)),
                    pl.BlockSpec((1, 1, nc, 128, TK), lambda b, g, i: (b, g, 0, 0, 0)),
                    pl.BlockSpec((1, 1, nc, TK, 128), lambda b, g, i: (b, g, 0, 0, 0))]
    else:
        body = functools.partial(_attn_d_kernel, n_chunks=nc, lam_init=lam_init)
        in_specs = [pl.BlockSpec((1, TQ, 128), lambda b, g, i: (b, i, g)),
                    pl.BlockSpec((1, 1, nc, 128, TK), lambda b, g, i: (b, g, 0, 0, 0)),
                    pl.BlockSpec((1, nc, TK, 128), lambda b, g, i: (b, 0, 0, g)),
                    _const_spec((1, D_QK)), _const_spec((1, D_QK)),
                    _const_spec((1, D_QK)), _const_spec((1, D_QK)),
                    _const_spec((1, 128))]
    return pl.pallas_call(
        body,
        grid=(bsz, 2, nq),
        in_specs=in_specs,
        out_specs=out_spec,
        out_shape=jax.ShapeDtypeStruct((bsz, s_len, 256), BF16),
        scratch_shapes=[pltpu.VMEM((nc, TQ, TK), F32)],
        compiler_params=pltpu.CompilerParams(
            dimension_semantics=("arbitrary", "arbitrary", "arbitrary"),
            vmem_limit_bytes=VMEM_LIMIT),
        name="attn_" + kind,
    )(q, kt, v, *extra)


def _attn_a_kernel(q_ref, k_ref, v_ref, sink_ref, o_ref, *, s_len):
    tile = pl.program_id(1)
    span = 3 * BLOCK
    lane = lax.broadcasted_iota(jnp.int32, (BLOCK, LANES), 1)
    sink = sink_ref[...]
    for j in range(TQA // BLOCK):
        blk = tile * (TQA // BLOCK) + j
        start = pl.multiple_of(jnp.clip((blk - 1) * BLOCK, 0, s_len - span), BLOCK)
        qpos = blk * BLOCK + lax.broadcasted_iota(jnp.int32, (BLOCK, span), 0)
        kpos = start + lax.broadcasted_iota(jnp.int32, (BLOCK, span), 1)
        dist = jnp.abs(qpos - kpos)
        valid = dist <= WINDOW
        distf = dist.astype(F32)
        for grp in range(2):
            kk = k_ref[0, grp, pl.ds(start, span), :]
            vv = v_ref[0, grp, pl.ds(start, span), :]
            qg = q_ref[0, j * BLOCK:(j + 1) * BLOCK, grp * 128:(grp + 1) * 128]
            outs = []
            for r in range(2):
                hd = 2 * grp + r
                qm = jnp.where((lane >= 64 * r) & (lane < 64 * (r + 1)), qg, jnp.zeros_like(qg))
                s = _dot_nt(qm, kk) - SLOPES[hd] * distf
                s = jnp.where(valid, s, -jnp.inf)
                sk = sink[:, hd:hd + 1]
                m = jnp.maximum(jnp.max(s, axis=-1, keepdims=True), sk)
                p = jnp.exp(s - m)
                den = jnp.sum(p, axis=-1, keepdims=True) + jnp.exp(sk - m)
                outs.append(_dot(p.astype(BF16), vv) / den)
            o_ref[0, j * BLOCK:(j + 1) * BLOCK, grp * 128:(grp + 1) * 128] = (
                jnp.where(lane < 64, outs[0], outs[1]).astype(BF16))


def _attn_a_call(q, k, v, sink):
    bsz, s_len = q.shape[0], q.shape[1]
    kv_spec = pl.BlockSpec((1, 2, s_len, 128), lambda b, i: (b, 0, 0, 0))
    return pl.pallas_call(
        functools.partial(_attn_a_kernel, s_len=s_len),
        grid=(bsz, s_len // TQA),
        in_specs=[pl.BlockSpec((1, TQA, 256), lambda b, i: (b, i, 0)), kv_spec, kv_spec,
                  _const_spec((1, A_HEADS))],
        out_specs=pl.BlockSpec((1, TQA, 256), lambda b, i: (b, i, 0)),
        out_shape=jax.ShapeDtypeStruct((bsz, s_len, 256), BF16),
        compiler_params=pltpu.CompilerParams(
            dimension_semantics=("arbitrary", "arbitrary"), vmem_limit_bytes=VMEM_LIMIT),
        name="attn_a",
    )(q, k, v, sink)


def _out_proj_kernel(x_ref, oa_ref, ob_ref, oc_ref, od_ref, wo_ref, g_ref, x1_ref, xn_ref):
    x1 = x_ref[0]
    for j, o_ref in enumerate((oa_ref, ob_ref, oc_ref, od_ref)):
        x1 = x1 + _dot(o_ref[0], wo_ref[j * 256:(j + 1) * 256, :])
    x1_ref[0] = x1
    xn_ref[0] = _rms(x1, g_ref[...]).astype(BF16)


def _out_proj_call(x, oa, ob, oc, od, lw):
    bsz, s_len, _ = x.shape
    tok = lambda w: pl.BlockSpec((1, TM, w), lambda b, i: (b, i, 0))
    return pl.pallas_call(
        _out_proj_kernel,
        grid=(bsz, s_len // TM),
        in_specs=[tok(D_MODEL), tok(256), tok(256), tok(256), tok(256),
                  _const_spec((D_MODEL, D_MODEL)), _const_spec((1, D_MODEL))],
        out_specs=(tok(D_MODEL), tok(D_MODEL)),
        out_shape=(jax.ShapeDtypeStruct((bsz, s_len, D_MODEL), F32),
                   jax.ShapeDtypeStruct((bsz, s_len, D_MODEL), BF16)),
        compiler_params=pltpu.CompilerParams(
            dimension_semantics=("arbitrary", "arbitrary"), vmem_limit_bytes=VMEM_LIMIT),
        name="out_proj",
    )(x, oa, ob, oc, od, lw["w_out"], lw["g_ffn"])


def _ffn_kernel(xn_ref, prev_ref, next_ref, x1_ref, wua_ref, wub_ref, vec_ref, wd_ref, gf_ref,
                o_ref, xext_ref, *, s_len, final_norm):
    t0 = pl.program_id(1) * TM
    xext_ref[0:HALO] = prev_ref[0]
    xext_ref[HALO:HALO + TM] = xn_ref[0]
    xext_ref[HALO + TM:HALO + TM + HALO] = next_ref[0]
    tpos = t0 - HALO + lax.broadcasted_iota(jnp.int32, (TM + 2 * HALO, 1), 0)
    valid = (tpos >= 0) & (tpos < s_len)

    def conv(hx, w0, w1, w2, b):
        return (w0 * hx[HALO - 1:HALO - 1 + TM] + w1 * hx[HALO:HALO + TM]
                + w2 * hx[HALO + 1:HALO + 1 + TM] + b)

    def chunk(c, acc):
        xe = xext_ref[...]
        vec = vec_ref[c]
        ha = jnp.where(valid, _dot(xe, wua_ref[c]) + vec[0:1], 0.0)
        hb = jnp.where(valid, _dot(xe, wub_ref[c]) + vec[1:2], 0.0)
        ca = conv(ha, vec[2:3], vec[3:4], vec[4:5], vec[8:9])
        cb = conv(hb, vec[5:6], vec[6:7], vec[7:8], vec[9:10])
        act = (ca / (1.0 + jnp.exp(-ca))) * cb
        return acc + _dot(act.astype(BF16), wd_ref[c])

    acc = lax.fori_loop(0, N_FF_CHUNKS, chunk, jnp.zeros((TM, D_MODEL), F32))
    out = x1_ref[0] + acc
    if final_norm:
        out = _rms(out, gf_ref[...])
    o_ref[0] = out


def _ffn_call(xn, x1, lw, g_final, final_norm):
    bsz, s_len, _ = x1.shape
    per_tile = TM // HALO
    last = s_len // HALO - 1
    tok = pl.BlockSpec((1, TM, D_MODEL), lambda b, i: (b, i, 0))
    prev = pl.BlockSpec((1, HALO, D_MODEL), lambda b, i: (b, jnp.maximum(i * per_tile - 1, 0), 0))
    nxt = pl.BlockSpec((1, HALO, D_MODEL),
                       lambda b, i: (b, jnp.minimum((i + 1) * per_tile, last), 0))
    return pl.pallas_call(
        functools.partial(_ffn_kernel, s_len=s_len, final_norm=final_norm),
        grid=(bsz, s_len // TM),
        in_specs=[tok, prev, nxt, tok,
                  _const_spec((N_FF_CHUNKS, D_MODEL, FF_CHUNK)),
                  _const_spec((N_FF_CHUNKS, D_MODEL, FF_CHUNK)),
                  _const_spec((N_FF_CHUNKS, 16, FF_CHUNK)),
                  _const_spec((N_FF_CHUNKS, FF_CHUNK, D_MODEL)),
                  _const_spec((1, D_MODEL))],
        out_specs=tok,
        out_shape=jax.ShapeDtypeStruct((bsz, s_len, D_MODEL), F32),
        scratch_shapes=[pltpu.VMEM((TM + 2 * HALO, D_MODEL), BF16)],
        compiler_params=pltpu.CompilerParams(
            dimension_semantics=("arbitrary", "arbitrary"), vmem_limit_bytes=VMEM_LIMIT),
        name="ffn",
    )(xn, xn, xn, x1, lw["w_up_a"], lw["w_up_b"], lw["ffn_vec"], lw["w_down"], g_final)


def _pack_layer(p, l):
    pieces = jnp.split(p["w_in"][l], np.cumsum(IN_SIZES)[:-1].tolist(), axis=1)
    aq, ak, av, bcq, bckv, bkr, cq, ck, cv, dq, dk, dv = pieces
    z = lambda n: jnp.zeros((D_MODEL, n), F32)
    w_in = jnp.concatenate([aq, ak, av, bcq, bckv, z(64), bkr, z(32), cq, ck, cv, dq, dk, dv], axis=1)

    wq = p["b_w_q_up"][l].reshape(B_Q_RANK, B_HEADS, B_NOPE + B_ROPE)
    wq = jnp.pad(wq, ((0, 0), (0, 0), (0, 128 - B_NOPE - B_ROPE))).reshape(B_Q_RANK, B_HEADS * 128)
    wkv = p["b_w_kv_up"][l].reshape(B_KV_RANK, B_HEADS, B_NOPE + B_V)
    wk = jnp.pad(wkv[:, :, :B_NOPE], ((0, 0), (0, 0), (0, 128 - B_NOPE))).reshape(B_KV_RANK, B_HEADS * 128)
    wv = wkv[:, :, B_NOPE:].reshape(B_KV_RANK, B_HEADS * B_V)

    w_up = p["w_up"][l]
    chunked = lambda w: w.reshape(D_MODEL, N_FF_CHUNKS, FF_CHUNK).transpose(1, 0, 2)
    halves = lambda v: (v[..., :D_FF], v[..., D_FF:])
    bua, bub = halves(p["b_up"][l])
    cwa, cwb = halves(p["conv_w"][l])
    cba, cbb = halves(p["conv_b"][l])
    rows = [bua, bub, cwa[0], cwa[1], cwa[2], cwb[0], cwb[1], cwb[2], cba, cbb]
    vec = jnp.stack(rows + [jnp.zeros_like(bua)] * (16 - len(rows)), axis=0)
    vec = vec.reshape(16, N_FF_CHUNKS, FF_CHUNK).transpose(1, 0, 2)
    row = lambda v: v.reshape(1, -1)
    return {
        "g_attn": row(p["g_attn"][l]),
        "w_in": w_in.astype(BF16),
        "a_sink": row(p["a_sink"][l]),
        "b_q_norm": row(p["b_q_norm"][l]),
        "w_q_up": wq.astype(BF16),
        "b_kv_norm": row(p["b_kv_norm"][l]),
        "w_kv_up": jnp.concatenate([wk, wv], axis=1).astype(BF16),
        "c_q_norm": row(jnp.tile(p["c_q_norm"][l], 4)),
        "c_k_norm": row(jnp.tile(p["c_k_norm"][l], 2)),
        "d_lambda": tuple(row(p[n][l]) for n in ("d_lambda_q1", "d_lambda_k1", "d_lambda_q2", "d_lambda_k2")),
        "d_sub_norm": row(jnp.tile(p["d_sub_norm"][l], 2)),
        "w_out": p["w_out"][l].astype(BF16),
        "g_ffn": row(p["g_ffn"][l]),
        "w_up_a": chunked(w_up[:, :D_FF]).astype(BF16),
        "w_up_b": chunked(w_up[:, D_FF:]).astype(BF16),
        "ffn_vec": vec,
        "w_down": p["w_down"][l].reshape(N_FF_CHUNKS, FF_CHUNK, D_MODEL).astype(BF16),
    }


def _rope_tables(s_len):
    half = 16
    inv = ROPE_THETA ** (-jnp.arange(half, dtype=F32) * 2.0 / (2 * half))
    pos = jnp.arange(s_len, dtype=F32)
    rows = s_len // GRID_W
    row_pos = jnp.broadcast_to(jnp.arange(rows, dtype=F32)[:, None], (rows, GRID_W)).reshape(s_len)
    col_pos = jnp.broadcast_to(jnp.arange(GRID_W, dtype=F32)[None, :], (rows, GRID_W)).reshape(s_len)

    def cs(p):
        ang = p[:, None] * inv[None, :]
        return jnp.cos(ang), jnp.sin(ang)

    one = lambda n: jnp.ones((s_len, n), F32)
    zero = lambda n: jnp.zeros((s_len, n), F32)
    c, s = cs(pos)
    rope_b = jnp.stack([
        jnp.concatenate([one(64), c, c, one(32)], axis=1),
        jnp.concatenate([zero(64), -s, zero(16), zero(32)], axis=1),
        jnp.concatenate([zero(64), zero(16), s, zero(32)], axis=1)])
    cr, sr = cs(row_pos)
    cc, sc = cs(col_pos)
    z = zero(16)
    rope_c = jnp.stack([
        jnp.tile(jnp.concatenate([cr, cr, cc, cc], axis=1), (1, 2)),
        jnp.tile(jnp.concatenate([-sr, z, -sc, z], axis=1), (1, 2)),
        jnp.tile(jnp.concatenate([z, sr, z, sc], axis=1), (1, 2))])
    return rope_b, rope_c


def _trunk(x, layers, g_final):
    bsz, s_len, _ = x.shape
    nc = s_len // TK
    rope_b, rope_c = _rope_tables(s_len)
    depth = len(layers)
    for l, lw in enumerate(layers):
        qa, ka, va, qb, ktb, vb, qc, ktc, vc, qd, ktd, vd = _proj_call(x, lw, rope_b, rope_c)
        oa = _attn_a_call(qa, ka, va, lw["a_sink"])
        ob = _full_attn_call("b", qb, ktb, vb.reshape(bsz, nc, TK, 256))
        oc = _full_attn_call("c", qc, ktc, vc.reshape(bsz, 2, nc, TK, 128))
        lam_init = 0.8 - 0.6 * math.exp(-0.3 * l)
        od = _full_attn_call("d", qd, ktd, vd.reshape(bsz, nc, TK, 256),
                             extra=(*lw["d_lambda"], lw["d_sub_norm"]), lam_init=lam_init)
        x1, xn = _out_proj_call(x, oa, ob, oc, od, lw)
        x = _ffn_call(xn, x1, lw, g_final, final_norm=(l == depth - 1))
    return x


def kernel(x_prompt, x_sample, g_attn, w_in, a_sink, b_q_norm, b_w_q_up, b_kv_norm, b_w_kv_up,
           c_q_norm, c_k_norm, d_lambda_q1, d_lambda_k1, d_lambda_q2, d_lambda_k2, d_sub_norm, w_out,
           g_ffn, w_up, b_up, conv_w, conv_b, w_down, g_final):
    p = dict(g_attn=g_attn, w_in=w_in, a_sink=a_sink, b_q_norm=b_q_norm, b_w_q_up=b_w_q_up,
             b_kv_norm=b_kv_norm, b_w_kv_up=b_w_kv_up, c_q_norm=c_q_norm, c_k_norm=c_k_norm,
             d_lambda_q1=d_lambda_q1, d_lambda_k1=d_lambda_k1, d_lambda_q2=d_lambda_q2,
             d_lambda_k2=d_lambda_k2, d_sub_norm=d_sub_norm, w_out=w_out, g_ffn=g_ffn, w_up=w_up,
             b_up=b_up, conv_w=conv_w, conv_b=conv_b, w_down=w_down)
    layers = [_pack_layer(p, l) for l in range(g_attn.shape[0])]
    gf = g_final.reshape(1, -1)
    return _trunk(x_prompt, layers, gf), _trunk(x_sample, layers, gf)
```

```python
import functools
import math

import jax
import jax.numpy as jnp
import numpy as np
from jax import lax
from jax.experimental import pallas as pl
from jax.experimental.pallas import tpu as pltpu

F32 = jnp.float32
BF16 = jnp.bfloat16

D_MODEL = 1024
GRID_W = 64
BLOCK = 128
HEAD_DIM = 64
EPS = 1e-6
ROPE_THETA = 10000.0

A_HEADS = 4
WINDOW = 128
B_HEADS = 4
B_Q_RANK = 256
B_KV_RANK = 128
B_NOPE = 64
B_ROPE = 32
B_V = 64
D_QK = 32
D_V = 64
N_ALIBI = 8
D_FF = 2816
IN_SIZES = (256, 128, 128, 256, 128, 32, 256, 128, 128, 256, 256, 256)

SLOPES = tuple(2.0 ** (-8.0 * (i + 1.0) / N_ALIBI) for i in range(N_ALIBI))
LOG2E = math.log2(math.e)

LANES = 128
TM = 512
TK = 512
TQ = 256
TQA = 512
HALO = 16
FF_CHUNK = 256
N_FF_CHUNKS = D_FF // FF_CHUNK
VMEM_LIMIT = 56 * 1024 * 1024

_OFF_AQ, _OFF_AK, _OFF_AV = 0, 256, 384
_OFF_BCQ, _OFF_BCKV, _OFF_BKR = 512, 768, 896
_OFF_CQ, _OFF_CK, _OFF_CV = 1024, 1280, 1408
_OFF_DQ, _OFF_DK, _OFF_DV = 1536, 1792, 2048
IN_PACKED = 2304


def _dot(a, b):
    return jnp.dot(a, b, preferred_element_type=F32)


def _dot_nt(a, b):
    return lax.dot_general(a, b, (((1,), (1,)), ((), ())), preferred_element_type=F32)


def _rms(x, g):
    return x * lax.rsqrt(jnp.mean(x * x, axis=-1, keepdims=True) + EPS) * g


def _rope(t, tab_ref):
    return (t * tab_ref[0] + pltpu.roll(t, LANES - 16, 1) * tab_ref[1]
            + pltpu.roll(t, 16, 1) * tab_ref[2])


def _dup_halves(t):
    lane = lax.broadcasted_iota(jnp.int32, t.shape, 1)
    sw = pltpu.roll(t, 64, 1)
    lo = lane < 64
    return jnp.where(lo, t, sw), jnp.where(lo, sw, t)


def _with_ones(t):
    lane = lax.broadcasted_iota(jnp.int32, t.shape, 1)
    lo = lane < 64
    return jnp.where(lo, t, 1.0), jnp.where(lo, pltpu.roll(t, 64, 1), 1.0)


def _head_mean_sq(t):
    w = t.shape[1]
    r = lax.broadcasted_iota(jnp.int32, (w, w), 0) // HEAD_DIM
    c = lax.broadcasted_iota(jnp.int32, (w, w), 1) // HEAD_DIM
    pool = jnp.where(r == c, 1.0 / HEAD_DIM, 0.0).astype(BF16)
    t2 = t * t
    hi = t2.astype(BF16)
    lo = (t2 - hi.astype(F32)).astype(BF16)
    return _dot(hi, pool) + _dot(lo, pool)


def _proj_kernel(x_ref, g_ref, win_ref, bqn_ref, wq_ref, bkvn_ref, wkv_ref, cqn_ref, ckn_ref,
                 ropeb_ref, ropec_ref,
                 qa_ref, ka_ref, va_ref, qb_ref, ktb_ref, vb_ref, qc_ref, ktc_ref, vc_ref,
                 qd_ref, ktd_ref, vd_ref):
    x = x_ref[0]
    xn = _rms(x, g_ref[...]).astype(BF16)
    h = _dot(xn, win_ref[...])

    qa_ref[0] = (h[:, _OFF_AQ:_OFF_AQ + 256] * (HEAD_DIM ** -0.5 * LOG2E)).astype(BF16)
    k0, k1 = _dup_halves(h[:, _OFF_AK:_OFF_AK + 128])
    ka_ref[0, 0] = k0.astype(BF16)
    ka_ref[0, 1] = k1.astype(BF16)
    v0, v1 = _dup_halves(h[:, _OFF_AV:_OFF_AV + 128])
    va_ref[0, 0] = v0.astype(BF16)
    va_ref[0, 1] = v1.astype(BF16)

    cq = _rms(h[:, _OFF_BCQ:_OFF_BCQ + B_Q_RANK], bqn_ref[...]).astype(BF16)
    qb = _dot(cq, wq_ref[...])
    scale_b = (B_NOPE + B_ROPE) ** -0.5 * LOG2E
    for hd in range(B_HEADS):
        t = _rope(qb[:, hd * 128:(hd + 1) * 128], ropeb_ref)
        qb_ref[0, :, hd * 128:(hd + 1) * 128] = (t * scale_b).astype(BF16)
    ckv = _rms(h[:, _OFF_BCKV:_OFF_BCKV + B_KV_RANK], bkvn_ref[...]).astype(BF16)
    kv = _dot(ckv, wkv_ref[...])
    kr = _rope(h[:, _OFF_BKR:_OFF_BKR + 128], ropeb_ref)
    for hd in range(B_HEADS):
        ktb_ref[0, hd, 0] = (kv[:, hd * 128:(hd + 1) * 128] + kr).T.astype(BF16)
    for grp in range(2):
        v0, v1 = _with_ones(kv[:, 512 + grp * 128:512 + (grp + 1) * 128])
        vb_ref[0, :, (2 * grp) * 128:(2 * grp + 1) * 128] = v0.astype(BF16)
        vb_ref[0, :, (2 * grp + 1) * 128:(2 * grp + 2) * 128] = v1.astype(BF16)

    cqh = h[:, _OFF_CQ:_OFF_CQ + 256]
    cqh = cqh * lax.rsqrt(_head_mean_sq(cqh) + EPS) * cqn_ref[...]
    for grp in range(2):
        t = _rope(cqh[:, grp * 128:(grp + 1) * 128], ropec_ref)
        qc_ref[0, :, grp * 128:(grp + 1) * 128] = (t * (HEAD_DIM ** -0.5 * LOG2E)).astype(BF16)
    ckh = h[:, _OFF_CK:_OFF_CK + 128]
    ckh = ckh * lax.rsqrt(_head_mean_sq(ckh) + EPS) * ckn_ref[...]
    ckh = _rope(ckh, ropec_ref)
    k0, k1 = _dup_halves(ckh)
    ktc_ref[0, 0, 0] = k0.T.astype(BF16)
    ktc_ref[0, 1, 0] = k1.T.astype(BF16)
    v0, v1 = _with_ones(h[:, _OFF_CV:_OFF_CV + 128])
    vc_ref[0, 0] = v0.astype(BF16)
    vc_ref[0, 1] = v1.astype(BF16)

    qd_ref[0] = (h[:, _OFF_DQ:_OFF_DQ + 256] * (D_QK ** -0.5 * LOG2E)).astype(BF16)
    for grp in range(2):
        ktd_ref[0, grp, 0] = h[:, _OFF_DK + grp * 128:_OFF_DK + (grp + 1) * 128].T.astype(BF16)
    for grp in range(2):
        v0, v1 = _with_ones(h[:, _OFF_DV + grp * 128:_OFF_DV + (grp + 1) * 128])
        vd_ref[0, :, (2 * grp) * 128:(2 * grp + 1) * 128] = v0.astype(BF16)
        vd_ref[0, :, (2 * grp + 1) * 128:(2 * grp + 2) * 128] = v1.astype(BF16)


def _const_spec(shape):
    nd = len(shape)
    return pl.BlockSpec(shape, lambda *_: (0,) * nd)


def _proj_call(x, lw, rope_b, rope_c):
    bsz, s_len, _ = x.shape
    nt = s_len // TM
    tok = lambda w: pl.BlockSpec((1, TM, w), lambda b, i: (b, i, 0))
    dup = pl.BlockSpec((1, 2, TM, 128), lambda b, i: (b, 0, i, 0))
    kt = lambda n: pl.BlockSpec((1, n, 1, 128, TK), lambda b, i: (b, 0, i, 0, 0))
    sds = jax.ShapeDtypeStruct
    out_shape = (
        sds((bsz, s_len, 256), BF16), sds((bsz, 2, s_len, 128), BF16), sds((bsz, 2, s_len, 128), BF16),
        sds((bsz, s_len, 512), BF16), sds((bsz, 4, nt, 128, TK), BF16), sds((bsz, s_len, 512), BF16),
        sds((bsz, s_len, 256), BF16), sds((bsz, 2, nt, 128, TK), BF16), sds((bsz, 2, s_len, 128), BF16),
        sds((bsz, s_len, 256), BF16), sds((bsz, 2, nt, 128, TK), BF16), sds((bsz, s_len, 512), BF16),
    )
    out_specs = (tok(256), dup, dup, tok(512), kt(4), tok(512),
                 tok(256), kt(2), dup, tok(256), kt(2), tok(512))
    rope_spec = pl.BlockSpec((3, TM, 128), lambda b, i: (0, i, 0))
    in_specs = [
        tok(D_MODEL), _const_spec((1, D_MODEL)), _const_spec((D_MODEL, IN_PACKED)),
        _const_spec((1, B_Q_RANK)), _const_spec((B_Q_RANK, 512)),
        _const_spec((1, B_KV_RANK)), _const_spec((B_KV_RANK, 768)),
        _const_spec((1, 256)), _const_spec((1, 128)),
        rope_spec, rope_spec,
    ]
    return pl.pallas_call(
        _proj_kernel,
        grid=(bsz, nt),
        in_specs=in_specs,
        out_specs=out_specs,
        out_shape=out_shape,
        compiler_params=pltpu.CompilerParams(
            dimension_semantics=("arbitrary", "arbitrary"), vmem_limit_bytes=VMEM_LIMIT),
        name="in_proj",
    )(x, lw["g_attn"], lw["w_in"], lw["b_q_norm"], lw["w_q_up"], lw["b_kv_norm"], lw["w_kv_up"],
      lw["c_q_norm"], lw["c_k_norm"], rope_b, rope_c)


def _fold(op, s):
    out = s[:, 0:LANES]
    for j in range(1, s.shape[1] // LANES):
        out = op(out, s[:, j * LANES:(j + 1) * LANES])
    return out


def _attend(streams, n_chunks, s_scr, bias_at=None, bias_scr=None):
    n = len(streams)
    rows = streams[0][0].shape[0]
    m_run = [jnp.full((rows, LANES), -jnp.inf, F32)] * n
    acc = [jnp.zeros((rows, LANES), F32)] * n
    m = [None] * n
    for k in range(n + 1):
        for c in range(n_chunks):
            if k < n:
                qm, kt_at, _ = streams[k]
                s = _dot(qm, kt_at(c))
                if bias_at is not None:
                    if k % 2 == 0:
                        bias = bias_at(k // 2, c)
                        bias_scr[c] = bias
                    else:
                        bias = bias_scr[c]
                    s = s - bias
                s_scr[k % 2, c] = s
                m_run[k] = jnp.maximum(m_run[k], _fold(jnp.maximum, s))
            if k >= 1:
                j = k - 1
                p = jnp.exp2((s_scr[j % 2, c] - m[j]).astype(BF16))
                acc[j] = acc[j] + _dot(p, streams[j][2](c))
        if k < n:
            m[k] = jnp.max(m_run[k], axis=-1, keepdims=True)
    return acc


def _normalised(acc):
    return acc / acc[:, 64:65]


def _pair_lanes(o0, o1):
    lane = lax.broadcasted_iota(jnp.int32, o0.shape, 1)
    return jnp.where(lane < 64, o0, pltpu.roll(o1, 64, 1))


def _attn_b_kernel(q_ref, kt_ref, v_ref, o_ref, s_scr, *, n_chunks):
    streams = [(q_ref[0, :, r * 128:(r + 1) * 128], lambda c, r=r: kt_ref[0, r, c],
                lambda c, r=r: v_ref[0, c, :, r * 128:(r + 1) * 128]) for r in range(2)]
    a0, a1 = _attend(streams, n_chunks, s_scr)
    o_ref[0] = _pair_lanes(_normalised(a0), _normalised(a1)).astype(BF16)


def _attn_c_kernel(q_ref, kt_ref, v_ref, o_ref, s_scr, *, n_chunks):
    lane = lax.broadcasted_iota(jnp.int32, (TQ, LANES), 1)
    q = q_ref[0]
    zero = jnp.zeros_like(q)
    kt_at = lambda c: kt_ref[0, 0, c]
    v_at = lambda c: v_ref[0, 0, c]
    streams = [(jnp.where(lane < 64, q, zero), kt_at, v_at), (jnp.where(lane < 64, zero, q), kt_at, v_at)]
    a0, a1 = _attend(streams, n_chunks, s_scr)
    o_ref[0] = _pair_lanes(_normalised(a0), _normalised(a1)).astype(BF16)


def _attn_d_kernel(q_ref, kt_ref, v_ref, lq1_ref, lk1_ref, lq2_ref, lk2_ref, sn_ref, o_ref, s_scr,
                   bias_scr, *, n_chunks, lam_init):
    grp = pl.program_id(1)
    q0 = pl.program_id(2) * TQ
    lam = (jnp.exp(jnp.sum(lq1_ref[...] * lk1_ref[...], axis=-1, keepdims=True))
           - jnp.exp(jnp.sum(lq2_ref[...] * lk2_ref[...], axis=-1, keepdims=True)) + lam_init)
    lane = lax.broadcasted_iota(jnp.int32, (TQ, LANES), 1)
    q = q_ref[0]
    rel = (lax.broadcasted_iota(jnp.int32, (TQ, TK), 0)
           - lax.broadcasted_iota(jnp.int32, (TQ, TK), 1)).astype(F32)
    slopes = [(jnp.where(grp == 0, SLOPES[A_HEADS + r], SLOPES[A_HEADS + 2 + r]) * LOG2E).astype(F32)
              for r in range(2)]

    def bias_at(r, c):
        return slopes[r] * jnp.abs(rel + (q0 - c * TK).astype(F32))

    streams = [(jnp.where((lane >= D_QK * i) & (lane < D_QK * (i + 1)), q, jnp.zeros_like(q)),
                lambda c: kt_ref[0, 0, c],
                lambda c, r=i // 2: v_ref[0, c, :, r * 128:(r + 1) * 128]) for i in range(4)]
    res = _attend(streams, n_chunks, s_scr, bias_at, bias_scr)
    outs = []
    for r in range(2):
        o = _normalised(res[2 * r]) - lam * _normalised(res[2 * r + 1])
        msq = jnp.sum(jnp.where(lane < 64, o * o, 0.0), axis=-1, keepdims=True) * (1.0 / D_V)
        outs.append(o * lax.rsqrt(msq + EPS) * sn_ref[...] * (1.0 - lam_init))
    o_ref[0] = _pair_lanes(outs[0], outs[1]).astype(BF16)


def _full_attn_call(kind, q, kt, v, extra=(), lam_init=0.0):
    bsz, s_len = q.shape[0], q.shape[1]
    nq, nc = s_len // TQ, s_len // TK
    out_spec = pl.BlockSpec((1, TQ, 128), lambda b, g, i: (b, i, g))
    if kind == "b":
        body = functools.partial(_attn_b_kernel, n_chunks=nc)
        in_specs = [pl.BlockSpec((1, TQ, 256), lambda b, g, i: (b, i, g)),
                    pl.BlockSpec((1, 2, nc, 128, TK), lambda b, g, i: (b, g, 0, 0, 0)),
                    pl.BlockSpec((1, nc, TK, 256), lambda b, g, i: (b, 0, 0, g))]
    elif kind == "c":
        body = functools.partial(_attn_c_kernel, n_chunks=nc)
        in_specs = [pl.BlockSpec((1, TQ, 128), lambda b, g, i: (b, i, g)),
                    pl.BlockSpec((1, 1, nc, 128, TK), lambda b, g, i: (b, g, 0, 0, 0)),
                    pl.BlockSpec((1, 1, nc, TK, 128), lambda b, g, i: (b, g, 0, 0, 0))]
    else:
        body = functools.partial(_attn_d_kernel, n_chunks=nc, lam_init=lam_init)
        in_specs = [pl.BlockSpec((1, TQ, 128), lambda b, g, i: (b, i, g)),
                    pl.BlockSpec((1, 1, nc, 128, TK), lambda b, g, i: (b, g, 0, 0, 0)),
                    pl.BlockSpec((1, nc, TK, 256), lambda b, g, i: (b, 0, 0, g)),
                    _const_spec((1, D_QK)), _const_spec((1, D_QK)),
                    _const_spec((1, D_QK)), _const_spec((1, D_QK)),
                    _const_spec((1, 128))]
    return pl.pallas_call(
        body,
        grid=(bsz, 2, nq),
        in_specs=in_specs,
        out_specs=out_spec,
        out_shape=jax.ShapeDtypeStruct((bsz, s_len, 256), BF16),
        scratch_shapes=[pltpu.VMEM((2, nc, TQ, TK), F32)]
        + ([pltpu.VMEM((nc, TQ, TK), F32)] if kind == "d" else []),
        compiler_params=pltpu.CompilerParams(
            dimension_semantics=("arbitrary", "arbitrary", "arbitrary"),
            vmem_limit_bytes=VMEM_LIMIT),
        name="attn_" + kind,
    )(q, kt, v, *extra)


def _attn_a_kernel(q_ref, k_ref, v_ref, sink_ref, o_ref, *, s_len):
    tile = pl.program_id(1)
    span = 3 * BLOCK
    lane = lax.broadcasted_iota(jnp.int32, (BLOCK, LANES), 1)
    sink = sink_ref[...]
    for j in range(TQA // BLOCK):
        blk = tile * (TQA // BLOCK) + j
        start = pl.multiple_of(jnp.clip((blk - 1) * BLOCK, 0, s_len - span), BLOCK)
        qpos = blk * BLOCK + lax.broadcasted_iota(jnp.int32, (BLOCK, span), 0)
        kpos = start + lax.broadcasted_iota(jnp.int32, (BLOCK, span), 1)
        dist = jnp.abs(qpos - kpos)
        valid = dist <= WINDOW
        distf = dist.astype(F32)
        for grp in range(2):
            kk = k_ref[0, grp, pl.ds(start, span), :]
            vv = v_ref[0, grp, pl.ds(start, span), :]
            qg = q_ref[0, j * BLOCK:(j + 1) * BLOCK, grp * 128:(grp + 1) * 128]
            outs = []
            for r in range(2):
                hd = 2 * grp + r
                qm = jnp.where((lane >= 64 * r) & (lane < 64 * (r + 1)), qg, jnp.zeros_like(qg))
                s = _dot_nt(qm, kk) - (SLOPES[hd] * LOG2E) * distf
                s = jnp.where(valid, s, -jnp.inf)
                sk = sink[:, hd:hd + 1] * LOG2E
                m = jnp.maximum(jnp.max(s, axis=-1, keepdims=True), sk)
                p = jnp.exp2(s - m)
                den = jnp.sum(p, axis=-1, keepdims=True) + jnp.exp2(sk - m)
                outs.append(_dot(p.astype(BF16), vv) / den)
            o_ref[0, j * BLOCK:(j + 1) * BLOCK, grp * 128:(grp + 1) * 128] = (
                jnp.where(lane < 64, outs[0], outs[1]).astype(BF16))


def _attn_a_call(q, k, v, sink):
    bsz, s_len = q.shape[0], q.shape[1]
    kv_spec = pl.BlockSpec((1, 2, s_len, 128), lambda b, i: (b, 0, 0, 0))
    return pl.pallas_call(
        functools.partial(_attn_a_kernel, s_len=s_len),
        grid=(bsz, s_len // TQA),
        in_specs=[pl.BlockSpec((1, TQA, 256), lambda b, i: (b, i, 0)), kv_spec, kv_spec,
                  _const_spec((1, A_HEADS))],
        out_specs=pl.BlockSpec((1, TQA, 256), lambda b, i: (b, i, 0)),
        out_shape=jax.ShapeDtypeStruct((bsz, s_len, 256), BF16),
        compiler_params=pltpu.CompilerParams(
            dimension_semantics=("arbitrary", "arbitrary"), vmem_limit_bytes=VMEM_LIMIT),
        name="attn_a",
    )(q, k, v, sink)


def _out_proj_kernel(x_ref, oa_ref, ob_ref, oc_ref, od_ref, wo_ref, g_ref, x1_ref, xn_ref):
    x1 = x_ref[0]
    for j, o_ref in enumerate((oa_ref, ob_ref, oc_ref, od_ref)):
        x1 = x1 + _dot(o_ref[0], wo_ref[j * 256:(j + 1) * 256, :])
    x1_ref[0] = x1
    xn_ref[0] = _rms(x1, g_ref[...]).astype(BF16)


def _out_proj_call(x, oa, ob, oc, od, lw):
    bsz, s_len, _ = x.shape
    tok = lambda w: pl.BlockSpec((1, TM, w), lambda b, i: (b, i, 0))
    return pl.pallas_call(
        _out_proj_kernel,
        grid=(bsz, s_len // TM),
        in_specs=[tok(D_MODEL), tok(256), tok(256), tok(256), tok(256),
                  _const_spec((D_MODEL, D_MODEL)), _const_spec((1, D_MODEL))],
        out_specs=(tok(D_MODEL), tok(D_MODEL)),
        out_shape=(jax.ShapeDtypeStruct((bsz, s_len, D_MODEL), F32),
                   jax.ShapeDtypeStruct((bsz, s_len, D_MODEL), BF16)),
        compiler_params=pltpu.CompilerParams(
            dimension_semantics=("arbitrary", "arbitrary"), vmem_limit_bytes=VMEM_LIMIT),
        name="out_proj",
    )(x, oa, ob, oc, od, lw["w_out"], lw["g_ffn"])


def _ffn_kernel(xn_ref, prev_ref, next_ref, x1_ref, wua_ref, wub_ref, vec_ref, wd_ref, gf_ref,
                o_ref, xext_ref, h0_ref, h1_ref, act_ref, *, final_norm):
    first = pl.program_id(1) == 0
    last = pl.program_id(1) == pl.num_programs(1) - 1
    xext_ref[0:HALO] = prev_ref[0]
    xext_ref[HALO:HALO + TM] = xn_ref[0]
    xext_ref[HALO + TM:HALO + TM + HALO] = next_ref[0]

    def up(c, h_ref):
        xe = xext_ref[...]
        h_ref[0] = _dot(xe, wua_ref[c])
        h_ref[1] = _dot(xe, wub_ref[c])

    def conv(h_ref, half, bias, w0, w1, w2, cb):
        lo = h_ref[half, HALO - 8:HALO, :]
        h_ref[half, HALO - 8:HALO, :] = jnp.where(first, -bias, lo)
        hi = h_ref[half, HALO + TM:HALO + TM + 8, :]
        h_ref[half, HALO + TM:HALO + TM + 8, :] = jnp.where(last, -bias, hi)
        return (w0 * h_ref[half, HALO - 1:HALO - 1 + TM, :] + w1 * h_ref[half, HALO:HALO + TM, :]
                + w2 * h_ref[half, HALO + 1:HALO + 1 + TM, :] + (bias * (w0 + w1 + w2) + cb))

    def gate(c, h_ref):
        vec = vec_ref[c]
        ca = conv(h_ref, 0, vec[0:1], vec[2:3], vec[3:4], vec[4:5], vec[8:9])
        cb = conv(h_ref, 1, vec[1:2], vec[5:6], vec[6:7], vec[7:8], vec[9:10])
        act_ref[c] = ((ca / (1.0 + jnp.exp(-ca))) * cb).astype(BF16)

    up(0, h0_ref)

    def pair(j, carry):
        up(2 * j + 1, h1_ref)
        gate(2 * j, h0_ref)
        up(2 * j + 2, h0_ref)
        gate(2 * j + 1, h1_ref)
        return carry

    lax.fori_loop(0, (N_FF_CHUNKS - 1) // 2, pair, 0)
    gate(N_FF_CHUNKS - 1, h0_ref)

    out = x1_ref[0]
    for c in range(N_FF_CHUNKS):
        out = out + _dot(act_ref[c], wd_ref[c])
    if final_norm:
        out = _rms(out, gf_ref[...])
    o_ref[0] = out


def _ffn_call(xn, x1, lw, g_final, final_norm):
    bsz, s_len, _ = x1.shape
    per_tile = TM // HALO
    last = s_len // HALO - 1
    tok = pl.BlockSpec((1, TM, D_MODEL), lambda b, i: (b, i, 0))
    prev = pl.BlockSpec((1, HALO, D_MODEL), lambda b, i: (b, jnp.maximum(i * per_tile - 1, 0), 0))
    nxt = pl.BlockSpec((1, HALO, D_MODEL),
                       lambda b, i: (b, jnp.minimum((i + 1) * per_tile, last), 0))
    return pl.pallas_call(
        functools.partial(_ffn_kernel, final_norm=final_norm),
        grid=(bsz, s_len // TM),
        in_specs=[tok, prev, nxt, tok,
                  _const_spec((N_FF_CHUNKS, D_MODEL, FF_CHUNK)),
                  _const_spec((N_FF_CHUNKS, D_MODEL, FF_CHUNK)),
                  _const_spec((N_FF_CHUNKS, 16, FF_CHUNK)),
                  _const_spec((N_FF_CHUNKS, FF_CHUNK, D_MODEL)),
                  _const_spec((1, D_MODEL))],
        out_specs=tok,
        out_shape=jax.ShapeDtypeStruct((bsz, s_len, D_MODEL), F32),
        scratch_shapes=[pltpu.VMEM((TM + 2 * HALO, D_MODEL), BF16),
                        pltpu.VMEM((2, TM + 2 * HALO, FF_CHUNK), F32),
                        pltpu.VMEM((2, TM + 2 * HALO, FF_CHUNK), F32),
                        pltpu.VMEM((N_FF_CHUNKS, TM, FF_CHUNK), BF16)],
        compiler_params=pltpu.CompilerParams(
            dimension_semantics=("arbitrary", "arbitrary"), vmem_limit_bytes=VMEM_LIMIT),
        name="ffn",
    )(xn, xn, xn, x1, lw["w_up_a"], lw["w_up_b"], lw["ffn_vec"], lw["w_down"], g_final)


def _pack_layer(p, l):
    pieces = jnp.split(p["w_in"][l], np.cumsum(IN_SIZES)[:-1].tolist(), axis=1)
    aq, ak, av, bcq, bckv, bkr, cq, ck, cv, dq, dk, dv = pieces
    z = lambda n: jnp.zeros((D_MODEL, n), F32)
    w_in = jnp.concatenate([aq, ak, av, bcq, bckv, z(64), bkr, z(32), cq, ck, cv, dq, dk, dv], axis=1)

    wq = p["b_w_q_up"][l].reshape(B_Q_RANK, B_HEADS, B_NOPE + B_ROPE)
    wq = jnp.pad(wq, ((0, 0), (0, 0), (0, 128 - B_NOPE - B_ROPE))).reshape(B_Q_RANK, B_HEADS * 128)
    wkv = p["b_w_kv_up"][l].reshape(B_KV_RANK, B_HEADS, B_NOPE + B_V)
    wk = jnp.pad(wkv[:, :, :B_NOPE], ((0, 0), (0, 0), (0, 128 - B_NOPE))).reshape(B_KV_RANK, B_HEADS * 128)
    wv = wkv[:, :, B_NOPE:].reshape(B_KV_RANK, B_HEADS * B_V)

    w_up = p["w_up"][l]
    chunked = lambda w: w.reshape(D_MODEL, N_FF_CHUNKS, FF_CHUNK).transpose(1, 0, 2)
    halves = lambda v: (v[..., :D_FF], v[..., D_FF:])
    bua, bub = halves(p["b_up"][l])
    cwa, cwb = halves(p["conv_w"][l])
    cba, cbb = halves(p["conv_b"][l])
    rows = [bua, bub, cwa[0], cwa[1], cwa[2], cwb[0], cwb[1], cwb[2], cba, cbb]
    vec = jnp.stack(rows + [jnp.zeros_like(bua)] * (16 - len(rows)), axis=0)
    vec = vec.reshape(16, N_FF_CHUNKS, FF_CHUNK).transpose(1, 0, 2)
    row = lambda v: v.reshape(1, -1)
    return {
        "g_attn": row(p["g_attn"][l]),
        "w_in": w_in.astype(BF16),
        "a_sink": row(p["a_sink"][l]),
        "b_q_norm": row(p["b_q_norm"][l]),
        "w_q_up": wq.astype(BF16),
        "b_kv_norm": row(p["b_kv_norm"][l]),
        "w_kv_up": jnp.concatenate([wk, wv], axis=1).astype(BF16),
        "c_q_norm": row(jnp.tile(p["c_q_norm"][l], 4)),
        "c_k_norm": row(jnp.tile(p["c_k_norm"][l], 2)),
        "d_lambda": tuple(row(p[n][l]) for n in ("d_lambda_q1", "d_lambda_k1", "d_lambda_q2", "d_lambda_k2")),
        "d_sub_norm": row(jnp.tile(p["d_sub_norm"][l], 2)),
        "w_out": p["w_out"][l].astype(BF16),
        "g_ffn": row(p["g_ffn"][l]),
        "w_up_a": chunked(w_up[:, :D_FF]).astype(BF16),
        "w_up_b": chunked(w_up[:, D_FF:]).astype(BF16),
        "ffn_vec": vec,
        "w_down": p["w_down"][l].reshape(N_FF_CHUNKS, FF_CHUNK, D_MODEL).astype(BF16),
    }


def _rope_tables(s_len):
    half = 16
    inv = ROPE_THETA ** (-jnp.arange(half, dtype=F32) * 2.0 / (2 * half))
    pos = jnp.arange(s_len, dtype=F32)
    rows = s_len // GRID_W
    row_pos = jnp.broadcast_to(jnp.arange(rows, dtype=F32)[:, None], (rows, GRID_W)).reshape(s_len)
    col_pos = jnp.broadcast_to(jnp.arange(GRID_W, dtype=F32)[None, :], (rows, GRID_W)).reshape(s_len)

    def cs(p):
        ang = p[:, None] * inv[None, :]
        return jnp.cos(ang), jnp.sin(ang)

    one = lambda n: jnp.ones((s_len, n), F32)
    zero = lambda n: jnp.zeros((s_len, n), F32)
    c, s = cs(pos)
    rope_b = jnp.stack([
        jnp.concatenate([one(64), c, c, one(32)], axis=1),
        jnp.concatenate([zero(64), -s, zero(16), zero(32)], axis=1),
        jnp.concatenate([zero(64), zero(16), s, zero(32)], axis=1)])
    cr, sr = cs(row_pos)
    cc, sc = cs(col_pos)
    z = zero(16)
    rope_c = jnp.stack([
        jnp.tile(jnp.concatenate([cr, cr, cc, cc], axis=1), (1, 2)),
        jnp.tile(jnp.concatenate([-sr, z, -sc, z], axis=1), (1, 2)),
        jnp.tile(jnp.concatenate([z, sr, z, sc], axis=1), (1, 2))])
    return rope_b, rope_c


def _trunk(x, layers, g_final):
    bsz, s_len, _ = x.shape
    nc = s_len // TK
    rope_b, rope_c = _rope_tables(s_len)
    depth = len(layers)
    for l, lw in enumerate(layers):
        qa, ka, va, qb, ktb, vb, qc, ktc, vc, qd, ktd, vd = _proj_call(x, lw, rope_b, rope_c)
        oa = _attn_a_call(qa, ka, va, lw["a_sink"])
        ob = _full_attn_call("b", qb, ktb, vb.reshape(bsz, nc, TK, 512))
        oc = _full_attn_call("c", qc, ktc, vc.reshape(bsz, 2, nc, TK, 128))
        lam_init = 0.8 - 0.6 * math.exp(-0.3 * l)
        od = _full_attn_call("d", qd, ktd, vd.reshape(bsz, nc, TK, 512),
                             extra=(*lw["d_lambda"], lw["d_sub_norm"]), lam_init=lam_init)
        x1, xn = _out_proj_call(x, oa, ob, oc, od, lw)
        x = _ffn_call(xn, x1, lw, g_final, final_norm=(l == depth - 1))
    return x


def kernel(x_prompt, x_sample, g_attn, w_in, a_sink, b_q_norm, b_w_q_up, b_kv_norm, b_w_kv_up,
           c_q_norm, c_k_norm, d_lambda_q1, d_lambda_k1, d_lambda_q2, d_lambda_k2, d_sub_norm, w_out,
           g_ffn, w_up, b_up, conv_w, conv_b, w_down, g_final):
    p = dict(g_attn=g_attn, w_in=w_in, a_sink=a_sink, b_q_norm=b_q_norm, b_w_q_up=b_w_q_up,
             b_kv_norm=b_kv_norm, b_w_kv_up=b_w_kv_up, c_q_norm=c_q_norm, c_k_norm=c_k_norm,
             d_lambda_q1=d_lambda_q1, d_lambda_k1=d_lambda_k1, d_lambda_q2=d_lambda_q2,
             d_lambda_k2=d_lambda_k2, d_sub_norm=d_sub_norm, w_out=w_out, g_ffn=g_ffn, w_up=w_up,
             b_up=b_up, conv_w=conv_w, conv_b=conv_b, w_down=w_down)
    layers = [_pack_layer(p, l) for l in range(g_attn.shape[0])]
    gf = g_final.reshape(1, -1)
    return _trunk(x_prompt, layers, gf), _trunk(x_sample, layers, gf)
```

```python
import functools
import math

import jax
import jax.numpy as jnp
import numpy as np
from jax import lax
from jax.experimental import pallas as pl
from jax.experimental.pallas import tpu as pltpu

F32 = jnp.float32
BF16 = jnp.bfloat16

D_MODEL = 1024
GRID_W = 64
BLOCK = 128
HEAD_DIM = 64
EPS = 1e-6
ROPE_THETA = 10000.0

A_HEADS = 4
WINDOW = 128
B_HEADS = 4
B_Q_RANK = 256
B_KV_RANK = 128
B_NOPE = 64
B_ROPE = 32
B_V = 64
D_QK = 32
D_V = 64
N_ALIBI = 8
D_FF = 2816
IN_SIZES = (256, 128, 128, 256, 128, 32, 256, 128, 128, 256, 256, 256)

SLOPES = tuple(2.0 ** (-8.0 * (i + 1.0) / N_ALIBI) for i in range(N_ALIBI))
LOG2E = math.log2(math.e)

LANES = 128
TM = 512
TK = 512
TQ = 256
TQA = 512
TKA = 256
N_CHUNKS_A = (TQA + 2 * WINDOW) // TKA
HALO = 16
FF_CHUNK = 256
N_FF_CHUNKS = D_FF // FF_CHUNK
VMEM_LIMIT = 56 * 1024 * 1024

_OFF_AQ, _OFF_AK, _OFF_AV = 0, 256, 384
_OFF_BCQ, _OFF_BCKV, _OFF_BKR = 512, 768, 896
_OFF_CQ, _OFF_CK, _OFF_CV = 1024, 1280, 1408
_OFF_DQ, _OFF_DK, _OFF_DV = 1536, 1792, 2048
IN_PACKED = 2304


def _dot(a, b):
    return jnp.dot(a, b, preferred_element_type=F32)


def _rms(x, g):
    return x * lax.rsqrt(jnp.mean(x * x, axis=-1, keepdims=True) + EPS) * g


def _rope(t, tab_ref):
    return (t * tab_ref[0] + pltpu.roll(t, LANES - 16, 1) * tab_ref[1]
            + pltpu.roll(t, 16, 1) * tab_ref[2])


def _dup_halves(t):
    lane = lax.broadcasted_iota(jnp.int32, t.shape, 1)
    sw = pltpu.roll(t, 64, 1)
    lo = lane < 64
    return jnp.where(lo, t, sw), jnp.where(lo, sw, t)


def _with_ones(t):
    lane = lax.broadcasted_iota(jnp.int32, t.shape, 1)
    lo = lane < 64
    return jnp.where(lo, t, 1.0), jnp.where(lo, pltpu.roll(t, 64, 1), 1.0)


def _head_mean_sq(t):
    w = t.shape[1]
    r = lax.broadcasted_iota(jnp.int32, (w, w), 0) // HEAD_DIM
    c = lax.broadcasted_iota(jnp.int32, (w, w), 1) // HEAD_DIM
    pool = jnp.where(r == c, 1.0 / HEAD_DIM, 0.0).astype(BF16)
    t2 = t * t
    hi = t2.astype(BF16)
    lo = (t2 - hi.astype(F32)).astype(BF16)
    return _dot(hi, pool) + _dot(lo, pool)


def _proj_kernel(x_ref, g_ref, win_ref, bqn_ref, wq_ref, bkvn_ref, wkv_ref, cqn_ref, ckn_ref,
                 ropeb_ref, ropec_ref,
                 qa_ref, ka_ref, va_ref, qb_ref, ktb_ref, vb_ref, qc_ref, ktc_ref, vc_ref,
                 qd_ref, ktd_ref, vd_ref):
    x = x_ref[0]
    xn = _rms(x, g_ref[...]).astype(BF16)
    h = _dot(xn, win_ref[...])

    qa_ref[0] = (h[:, _OFF_AQ:_OFF_AQ + 256] * (HEAD_DIM ** -0.5 * LOG2E)).astype(BF16)
    for grp, kd in enumerate(_dup_halves(h[:, _OFF_AK:_OFF_AK + 128])):
        for j in range(TM // BLOCK):
            ka_ref[0, grp, j] = kd[j * BLOCK:(j + 1) * BLOCK].T.astype(BF16)
    v0, v1 = _with_ones(h[:, _OFF_AV:_OFF_AV + 128])
    va_ref[0, 0] = v0.astype(BF16)
    va_ref[0, 1] = v1.astype(BF16)

    cq = _rms(h[:, _OFF_BCQ:_OFF_BCQ + B_Q_RANK], bqn_ref[...]).astype(BF16)
    qb = _dot(cq, wq_ref[...])
    scale_b = (B_NOPE + B_ROPE) ** -0.5 * LOG2E
    for hd in range(B_HEADS):
        t = _rope(qb[:, hd * 128:(hd + 1) * 128], ropeb_ref)
        qb_ref[0, :, hd * 128:(hd + 1) * 128] = (t * scale_b).astype(BF16)
    ckv = _rms(h[:, _OFF_BCKV:_OFF_BCKV + B_KV_RANK], bkvn_ref[...]).astype(BF16)
    kv = _dot(ckv, wkv_ref[...])
    kr = _rope(h[:, _OFF_BKR:_OFF_BKR + 128], ropeb_ref)
    for hd in range(B_HEADS):
        ktb_ref[0, hd, 0] = (kv[:, hd * 128:(hd + 1) * 128] + kr).T.astype(BF16)
    for grp in range(2):
        v0, v1 = _with_ones(kv[:, 512 + grp * 128:512 + (grp + 1) * 128])
        vb_ref[0, :, (2 * grp) * 128:(2 * grp + 1) * 128] = v0.astype(BF16)
        vb_ref[0, :, (2 * grp + 1) * 128:(2 * grp + 2) * 128] = v1.astype(BF16)

    cqh = h[:, _OFF_CQ:_OFF_CQ + 256]
    cqh = cqh * lax.rsqrt(_head_mean_sq(cqh) + EPS) * cqn_ref[...]
    for grp in range(2):
        t = _rope(cqh[:, grp * 128:(grp + 1) * 128], ropec_ref)
        qc_ref[0, :, grp * 128:(grp + 1) * 128] = (t * (HEAD_DIM ** -0.5 * LOG2E)).astype(BF16)
    ckh = h[:, _OFF_CK:_OFF_CK + 128]
    ckh = ckh * lax.rsqrt(_head_mean_sq(ckh) + EPS) * ckn_ref[...]
    ckh = _rope(ckh, ropec_ref)
    k0, k1 = _dup_halves(ckh)
    ktc_ref[0, 0, 0] = k0.T.astype(BF16)
    ktc_ref[0, 1, 0] = k1.T.astype(BF16)
    v0, v1 = _with_ones(h[:, _OFF_CV:_OFF_CV + 128])
    vc_ref[0, 0] = v0.astype(BF16)
    vc_ref[0, 1] = v1.astype(BF16)

    qd_ref[0] = (h[:, _OFF_DQ:_OFF_DQ + 256] * (D_QK ** -0.5 * LOG2E)).astype(BF16)
    for grp in range(2):
        ktd_ref[0, grp, 0] = h[:, _OFF_DK + grp * 128:_OFF_DK + (grp + 1) * 128].T.astype(BF16)
    for grp in range(2):
        v0, v1 = _with_ones(h[:, _OFF_DV + grp * 128:_OFF_DV + (grp + 1) * 128])
        vd_ref[0, :, (2 * grp) * 128:(2 * grp + 1) * 128] = v0.astype(BF16)
        vd_ref[0, :, (2 * grp + 1) * 128:(2 * grp + 2) * 128] = v1.astype(BF16)


def _const_spec(shape):
    nd = len(shape)
    return pl.BlockSpec(shape, lambda *_: (0,) * nd)


def _per_batch_spec(a):
    return pl.BlockSpec((1,) + a.shape[1:], lambda b, i: (b,) + (0,) * (a.ndim - 1))


def _proj_call(x, lw, rope_b, rope_c):
    bsz, s_len, _ = x.shape
    nt = s_len // TM
    tok = lambda w: pl.BlockSpec((1, TM, w), lambda b, i: (b, i, 0))
    dup = pl.BlockSpec((1, 2, TM, 128), lambda b, i: (b, 0, i, 0))
    kt = lambda n: pl.BlockSpec((1, n, 1, 128, TK), lambda b, i: (b, 0, i, 0, 0))
    sds = jax.ShapeDtypeStruct
    out_shape = (
        sds((bsz, s_len, 256), BF16), sds((bsz, 2, s_len // BLOCK, 128, BLOCK), BF16),
        sds((bsz, 2, s_len, 128), BF16),
        sds((bsz, s_len, 512), BF16), sds((bsz, 4, nt, 128, TK), BF16), sds((bsz, s_len, 512), BF16),
        sds((bsz, s_len, 256), BF16), sds((bsz, 2, nt, 128, TK), BF16), sds((bsz, 2, s_len, 128), BF16),
        sds((bsz, s_len, 256), BF16), sds((bsz, 2, nt, 128, TK), BF16), sds((bsz, s_len, 512), BF16),
    )
    kta = pl.BlockSpec((1, 2, TM // BLOCK, 128, BLOCK), lambda b, i: (b, 0, i, 0, 0))
    out_specs = (tok(256), kta, dup, tok(512), kt(4), tok(512),
                 tok(256), kt(2), dup, tok(256), kt(2), tok(512))
    rope_spec = pl.BlockSpec((3, TM, 128), lambda b, i: (0, i, 0))
    in_specs = [
        tok(D_MODEL), _const_spec((1, D_MODEL)), _const_spec((D_MODEL, IN_PACKED)),
        _const_spec((1, B_Q_RANK)), _const_spec((B_Q_RANK, 512)),
        _const_spec((1, B_KV_RANK)), _const_spec((B_KV_RANK, 768)),
        _const_spec((1, 256)), _const_spec((1, 128)),
        rope_spec, rope_spec,
    ]
    return pl.pallas_call(
        _proj_kernel,
        grid=(bsz, nt),
        in_specs=in_specs,
        out_specs=out_specs,
        out_shape=out_shape,
        compiler_params=pltpu.CompilerParams(
            dimension_semantics=("arbitrary", "arbitrary"), vmem_limit_bytes=VMEM_LIMIT),
        name="in_proj",
    )(x, lw["g_attn"], lw["w_in"], lw["b_q_norm"], lw["w_q_up"], lw["b_kv_norm"], lw["w_kv_up"],
      lw["c_q_norm"], lw["c_k_norm"], rope_b, rope_c)


def _fold(op, s):
    out = s[:, 0:LANES]
    for j in range(1, s.shape[1] // LANES):
        out = op(out, s[:, j * LANES:(j + 1) * LANES])
    return out


def _attend(streams, n_chunks, s_scr, add_at=None, m_floor=None):
    n = len(streams)
    rows = streams[0][0].shape[0]
    m_run = [jnp.full((rows, LANES), -jnp.inf, F32)] * n
    acc = [jnp.zeros((rows, LANES), F32)] * n
    m = [None] * n
    for k in range(n + 1):
        for c in range(n_chunks):
            if k < n:
                qm, kt_at, _ = streams[k]
                s = _dot(qm, kt_at(c))
                if add_at is not None:
                    s = s + add_at(k, c)
                s_scr[k % 2, c] = s
                m_run[k] = jnp.maximum(m_run[k], _fold(jnp.maximum, s))
            if k >= 1:
                j = k - 1
                p = jnp.exp2((s_scr[j % 2, c] - m[j]).astype(BF16))
                acc[j] = acc[j] + _dot(p, streams[j][2](c))
        if k < n:
            m[k] = jnp.max(m_run[k], axis=-1, keepdims=True)
            if m_floor is not None:
                m[k] = jnp.maximum(m[k], m_floor[k])
    return acc, m


def _normalised(acc):
    return acc / acc[:, 64:65]


def _pair_lanes(o0, o1):
    lane = lax.broadcasted_iota(jnp.int32, o0.shape, 1)
    return jnp.where(lane < 64, o0, pltpu.roll(o1, 64, 1))


def _half_masks(q):
    lane = lax.broadcasted_iota(jnp.int32, q.shape, 1)
    zero = jnp.zeros_like(q)
    return jnp.where(lane < 64, q, zero), jnp.where(lane < 64, zero, q)


def _attn_b_kernel(q_ref, kt_ref, v_ref, o_ref, s_scr, *, n_chunks):
    streams = [(q_ref[0, :, h * 128:(h + 1) * 128], lambda c, h=h: kt_ref[0, h, c],
                lambda c, h=h: v_ref[0, c, :, h * 128:(h + 1) * 128]) for h in range(B_HEADS)]
    acc, _ = _attend(streams, n_chunks, s_scr)
    for g in range(2):
        o_ref[0, :, g * 128:(g + 1) * 128] = _pair_lanes(
            _normalised(acc[2 * g]), _normalised(acc[2 * g + 1])).astype(BF16)


def _attn_c_kernel(q_ref, kt_ref, v_ref, o_ref, s_scr, *, n_chunks):
    streams = []
    for g in range(2):
        for qm in _half_masks(q_ref[0, :, g * 128:(g + 1) * 128]):
            streams.append((qm, lambda c, g=g: kt_ref[0, g, c], lambda c, g=g: v_ref[0, g, c]))
    acc, _ = _attend(streams, n_chunks, s_scr)
    for g in range(2):
        o_ref[0, :, g * 128:(g + 1) * 128] = _pair_lanes(
            _normalised(acc[2 * g]), _normalised(acc[2 * g + 1])).astype(BF16)


def _attn_d_kernel(q_ref, kt_ref, v_ref, lq1_ref, lk1_ref, lq2_ref, lk2_ref, sn_ref, o_ref, s_scr,
                   bias_scr, *, n_chunks, lam_init):
    grp = pl.program_id(1)
    q0 = pl.program_id(2) * TQ
    lam = (jnp.exp(jnp.sum(lq1_ref[...] * lk1_ref[...], axis=-1, keepdims=True))
           - jnp.exp(jnp.sum(lq2_ref[...] * lk2_ref[...], axis=-1, keepdims=True)) + lam_init)
    lane = lax.broadcasted_iota(jnp.int32, (TQ, LANES), 1)
    slopes = [(jnp.where(grp == 0, SLOPES[A_HEADS + r], SLOPES[A_HEADS + 2 + r]) * LOG2E).astype(F32)
              for r in range(2)]

    rel = (lax.broadcasted_iota(jnp.int32, (TQ, TK), 0)
           - lax.broadcasted_iota(jnp.int32, (TQ, TK), 1)).astype(F32)

    def add_at(k, c):
        if k % 2 == 1:
            return bias_scr[c]
        bias = -slopes[k // 2] * jnp.abs(rel + (q0 - c * TK).astype(F32))
        bias_scr[c] = bias
        return bias

    q = q_ref[0]
    streams = [(jnp.where((lane >= D_QK * i) & (lane < D_QK * (i + 1)), q, jnp.zeros_like(q)),
                lambda c: kt_ref[0, 0, c],
                lambda c, r=i // 2: v_ref[0, c, :, r * 128:(r + 1) * 128]) for i in range(4)]
    acc, _ = _attend(streams, n_chunks, s_scr, add_at)
    outs = []
    for r in range(2):
        o = _normalised(acc[2 * r]) - lam * _normalised(acc[2 * r + 1])
        msq = jnp.sum(jnp.where(lane < 64, o * o, 0.0), axis=-1, keepdims=True) * (1.0 / D_V)
        outs.append(o * lax.rsqrt(msq + EPS) * sn_ref[...] * (1.0 - lam_init))
    o_ref[0] = _pair_lanes(outs[0], outs[1]).astype(BF16)


def _full_attn_call(kind, q, kt, v, extra=(), lam_init=0.0):
    bsz, s_len = q.shape[0], q.shape[1]
    nq, nc = s_len // TQ, s_len // TK
    tok = lambda w: pl.BlockSpec((1, TQ, w), lambda b, i: (b, i, 0))
    scratch = [pltpu.VMEM((2, nc, TQ, TK), F32)]
    if kind == "b":
        body = functools.partial(_attn_b_kernel, n_chunks=nc)
        in_specs = [tok(512), _per_batch_spec(kt), _per_batch_spec(v)]
    elif kind == "c":
        body = functools.partial(_attn_c_kernel, n_chunks=nc)
        in_specs = [tok(256), _per_batch_spec(kt), _per_batch_spec(v)]
    else:
        return pl.pallas_call(
            functools.partial(_attn_d_kernel, n_chunks=nc, lam_init=lam_init),
            grid=(bsz, 2, nq),
            in_specs=[pl.BlockSpec((1, TQ, 128), lambda b, g, i: (b, i, g)),
                      pl.BlockSpec((1, 1, nc, 128, TK), lambda b, g, i: (b, g, 0, 0, 0)),
                      pl.BlockSpec((1, nc, TK, 256), lambda b, g, i: (b, 0, 0, g)),
                      _const_spec((1, D_QK)), _const_spec((1, D_QK)),
                      _const_spec((1, D_QK)), _const_spec((1, D_QK)),
                      _const_spec((1, 128))],
            out_specs=pl.BlockSpec((1, TQ, 128), lambda b, g, i: (b, i, g)),
            out_shape=jax.ShapeDtypeStruct((bsz, s_len, 256), BF16),
            scratch_shapes=scratch + [pltpu.VMEM((nc, TQ, TK), F32)],
            compiler_params=pltpu.CompilerParams(
                dimension_semantics=("arbitrary", "arbitrary", "arbitrary"),
                vmem_limit_bytes=VMEM_LIMIT),
            name="attn_d",
        )(q, kt, v, *extra)
    return pl.pallas_call(
        body,
        grid=(bsz, nq),
        in_specs=in_specs,
        out_specs=tok(256),
        out_shape=jax.ShapeDtypeStruct((bsz, s_len, 256), BF16),
        scratch_shapes=scratch,
        compiler_params=pltpu.CompilerParams(
            dimension_semantics=("arbitrary", "arbitrary"), vmem_limit_bytes=VMEM_LIMIT),
        name="attn_" + kind,
    )(q, kt, v, *extra)


def _attn_a_kernel(q_ref, kt_ref, v_ref, sink_ref, o_ref, s_scr, *, s_len):
    tile = pl.program_id(1)
    q0 = tile * TQA
    n_blocks = s_len // BLOCK
    per_chunk = TKA // BLOCK
    first = tile * (TQA // BLOCK) - 1

    def key_blocks(c):
        return [jnp.clip(first + per_chunk * c + j, 0, n_blocks - 1) for j in range(per_chunk)]

    def kt_at(g, c):
        return jnp.concatenate([kt_ref[0, g, i] for i in key_blocks(c)], axis=1)

    def v_at(g, c):
        return jnp.concatenate(
            [v_ref[0, g, pl.ds(pl.multiple_of(i * BLOCK, BLOCK), BLOCK), :] for i in key_blocks(c)], axis=0)

    neg_dist = []
    for c in range(N_CHUNKS_A):
        col = lax.broadcasted_iota(jnp.int32, (TQA, TKA), 1)
        diff = lax.broadcasted_iota(jnp.int32, (TQA, TKA), 0) - col + (BLOCK - TKA * c)
        kpos = q0 - BLOCK + TKA * c + col
        valid = (jnp.abs(diff) <= WINDOW) & (kpos >= 0) & (kpos < s_len)
        neg_dist.append(jnp.where(valid, -jnp.abs(diff).astype(F32), -jnp.inf))

    streams = []
    for g in range(2):
        for qm in _half_masks(q_ref[0, :, g * 128:(g + 1) * 128]):
            streams.append((qm, functools.partial(kt_at, g), functools.partial(v_at, g)))
    sinks = [sink_ref[:, h:h + 1] * LOG2E for h in range(A_HEADS)]
    acc, m = _attend(streams, N_CHUNKS_A, s_scr,
                     add_at=lambda k, c: (SLOPES[k] * LOG2E) * neg_dist[c], m_floor=sinks)
    outs = []
    for h in range(A_HEADS):
        den = acc[h] + jnp.exp2(sinks[h] - m[h])
        outs.append(acc[h] / den[:, 64:65])
    for g in range(2):
        o_ref[0, :, g * 128:(g + 1) * 128] = _pair_lanes(outs[2 * g], outs[2 * g + 1]).astype(BF16)


def _attn_a_call(q, kt, v, sink):
    bsz, s_len = q.shape[0], q.shape[1]
    tok = pl.BlockSpec((1, TQA, 256), lambda b, i: (b, i, 0))
    return pl.pallas_call(
        functools.partial(_attn_a_kernel, s_len=s_len),
        grid=(bsz, s_len // TQA),
        in_specs=[tok, _per_batch_spec(kt), _per_batch_spec(v), _const_spec((1, A_HEADS))],
        out_specs=tok,
        out_shape=jax.ShapeDtypeStruct((bsz, s_len, 256), BF16),
        scratch_shapes=[pltpu.VMEM((2, N_CHUNKS_A, TQA, TKA), F32)],
        compiler_params=pltpu.CompilerParams(
            dimension_semantics=("arbitrary", "arbitrary"), vmem_limit_bytes=VMEM_LIMIT),
        name="attn_a",
    )(q, kt, v, sink)


def _out_proj_kernel(x_ref, oa_ref, ob_ref, oc_ref, od_ref, wo_ref, g_ref, x1_ref, xn_ref):
    x1 = x_ref[0]
    for j, o_ref in enumerate((oa_ref, ob_ref, oc_ref, od_ref)):
        x1 = x1 + _dot(o_ref[0], wo_ref[j * 256:(j + 1) * 256, :])
    x1_ref[0] = x1
    xn_ref[0] = _rms(x1, g_ref[...]).astype(BF16)


def _out_proj_call(x, oa, ob, oc, od, lw):
    bsz, s_len, _ = x.shape
    tok = lambda w: pl.BlockSpec((1, TM, w), lambda b, i: (b, i, 0))
    return pl.pallas_call(
        _out_proj_kernel,
        grid=(bsz, s_len // TM),
        in_specs=[tok(D_MODEL), tok(256), tok(256), tok(256), tok(256),
                  _const_spec((D_MODEL, D_MODEL)), _const_spec((1, D_MODEL))],
        out_specs=(tok(D_MODEL), tok(D_MODEL)),
        out_shape=(jax.ShapeDtypeStruct((bsz, s_len, D_MODEL), F32),
                   jax.ShapeDtypeStruct((bsz, s_len, D_MODEL), BF16)),
        compiler_params=pltpu.CompilerParams(
            dimension_semantics=("arbitrary", "arbitrary"), vmem_limit_bytes=VMEM_LIMIT),
        name="out_proj",
    )(x, oa, ob, oc, od, lw["w_out"], lw["g_ffn"])


def _ffn_kernel(xn_ref, prev_ref, next_ref, x1_ref, wua_ref, wub_ref, vec_ref, wd_ref, gf_ref,
                o_ref, xext_ref, h0_ref, h1_ref, act_ref, *, final_norm):
    first = pl.program_id(1) == 0
    last = pl.program_id(1) == pl.num_programs(1) - 1
    xext_ref[0:HALO] = prev_ref[0]
    xext_ref[HALO:HALO + TM] = xn_ref[0]
    xext_ref[HALO + TM:HALO + TM + HALO] = next_ref[0]

    def up(c, h_ref):
        xe = xext_ref[...]
        h_ref[0] = _dot(xe, wua_ref[c])
        h_ref[1] = _dot(xe, wub_ref[c])

    def conv(h_ref, half, bias, w0, w1, w2, cb):
        lo = h_ref[half, HALO - 8:HALO, :]
        h_ref[half, HALO - 8:HALO, :] = jnp.where(first, -bias, lo)
        hi = h_ref[half, HALO + TM:HALO + TM + 8, :]
        h_ref[half, HALO + TM:HALO + TM + 8, :] = jnp.where(last, -bias, hi)
        return (w0 * h_ref[half, HALO - 1:HALO - 1 + TM, :] + w1 * h_ref[half, HALO:HALO + TM, :]
                + w2 * h_ref[half, HALO + 1:HALO + 1 + TM, :] + (bias * (w0 + w1 + w2) + cb))

    def gate(c, h_ref):
        vec = vec_ref[c]
        ca = conv(h_ref, 0, vec[0:1], vec[2:3], vec[3:4], vec[4:5], vec[8:9])
        cb = conv(h_ref, 1, vec[1:2], vec[5:6], vec[6:7], vec[7:8], vec[9:10])
        act_ref[c] = ((ca / (1.0 + jnp.exp(-ca))) * cb).astype(BF16)

    up(0, h0_ref)

    def pair(j, carry):
        up(2 * j + 1, h1_ref)
        gate(2 * j, h0_ref)
        up(2 * j + 2, h0_ref)
        gate(2 * j + 1, h1_ref)
        return carry

    lax.fori_loop(0, (N_FF_CHUNKS - 1) // 2, pair, 0)
    gate(N_FF_CHUNKS - 1, h0_ref)

    out = x1_ref[0]
    for c in range(N_FF_CHUNKS):
        out = out + _dot(act_ref[c], wd_ref[c])
    if final_norm:
        out = _rms(out, gf_ref[...])
    o_ref[0] = out


def _ffn_call(xn, x1, lw, g_final, final_norm):
    bsz, s_len, _ = x1.shape
    per_tile = TM // HALO
    last = s_len // HALO - 1
    tok = pl.BlockSpec((1, TM, D_MODEL), lambda b, i: (b, i, 0))
    prev = pl.BlockSpec((1, HALO, D_MODEL), lambda b, i: (b, jnp.maximum(i * per_tile - 1, 0), 0))
    nxt = pl.BlockSpec((1, HALO, D_MODEL),
                       lambda b, i: (b, jnp.minimum((i + 1) * per_tile, last), 0))
    return pl.pallas_call(
        functools.partial(_ffn_kernel, final_norm=final_norm),
        grid=(bsz, s_len // TM),
        in_specs=[tok, prev, nxt, tok,
                  _const_spec((N_FF_CHUNKS, D_MODEL, FF_CHUNK)),
                  _const_spec((N_FF_CHUNKS, D_MODEL, FF_CHUNK)),
                  _const_spec((N_FF_CHUNKS, 16, FF_CHUNK)),
                  _const_spec((N_FF_CHUNKS, FF_CHUNK, D_MODEL)),
                  _const_spec((1, D_MODEL))],
        out_specs=tok,
        out_shape=jax.ShapeDtypeStruct((bsz, s_len, D_MODEL), F32),
        scratch_shapes=[pltpu.VMEM((TM + 2 * HALO, D_MODEL), BF16),
                        pltpu.VMEM((2, TM + 2 * HALO, FF_CHUNK), F32),
                        pltpu.VMEM((2, TM + 2 * HALO, FF_CHUNK), F32),
                        pltpu.VMEM((N_FF_CHUNKS, TM, FF_CHUNK), BF16)],
        compiler_params=pltpu.CompilerParams(
            dimension_semantics=("arbitrary", "arbitrary"), vmem_limit_bytes=VMEM_LIMIT),
        name="ffn",
    )(xn, xn, xn, x1, lw["w_up_a"], lw["w_up_b"], lw["ffn_vec"], lw["w_down"], g_final)


def _pack_layer(p, l):
    pieces = jnp.split(p["w_in"][l], np.cumsum(IN_SIZES)[:-1].tolist(), axis=1)
    aq, ak, av, bcq, bckv, bkr, cq, ck, cv, dq, dk, dv = pieces
    z = lambda n: jnp.zeros((D_MODEL, n), F32)
    w_in = jnp.concatenate([aq, ak, av, bcq, bckv, z(64), bkr, z(32), cq, ck, cv, dq, dk, dv], axis=1)

    wq = p["b_w_q_up"][l].reshape(B_Q_RANK, B_HEADS, B_NOPE + B_ROPE)
    wq = jnp.pad(wq, ((0, 0), (0, 0), (0, 128 - B_NOPE - B_ROPE))).reshape(B_Q_RANK, B_HEADS * 128)
    wkv = p["b_w_kv_up"][l].reshape(B_KV_RANK, B_HEADS, B_NOPE + B_V)
    wk = jnp.pad(wkv[:, :, :B_NOPE], ((0, 0), (0, 0), (0, 128 - B_NOPE))).reshape(B_KV_RANK, B_HEADS * 128)
    wv = wkv[:, :, B_NOPE:].reshape(B_KV_RANK, B_HEADS * B_V)

    w_up = p["w_up"][l]
    chunked = lambda w: w.reshape(D_MODEL, N_FF_CHUNKS, FF_CHUNK).transpose(1, 0, 2)
    halves = lambda v: (v[..., :D_FF], v[..., D_FF:])
    bua, bub = halves(p["b_up"][l])
    cwa, cwb = halves(p["conv_w"][l])
    cba, cbb = halves(p["conv_b"][l])
    rows = [bua, bub, cwa[0], cwa[1], cwa[2], cwb[0], cwb[1], cwb[2], cba, cbb]
    vec = jnp.stack(rows + [jnp.zeros_like(bua)] * (16 - len(rows)), axis=0)
    vec = vec.reshape(16, N_FF_CHUNKS, FF_CHUNK).transpose(1, 0, 2)
    row = lambda v: v.reshape(1, -1)
    return {
        "g_attn": row(p["g_attn"][l]),
        "w_in": w_in.astype(BF16),
        "a_sink": row(p["a_sink"][l]),
        "b_q_norm": row(p["b_q_norm"][l]),
        "w_q_up": wq.astype(BF16),
        "b_kv_norm": row(p["b_kv_norm"][l]),
        "w_kv_up": jnp.concatenate([wk, wv], axis=1).astype(BF16),
        "c_q_norm": row(jnp.tile(p["c_q_norm"][l], 4)),
        "c_k_norm": row(jnp.tile(p["c_k_norm"][l], 2)),
        "d_lambda": tuple(row(p[n][l]) for n in ("d_lambda_q1", "d_lambda_k1", "d_lambda_q2", "d_lambda_k2")),
        "d_sub_norm": row(jnp.tile(p["d_sub_norm"][l], 2)),
        "w_out": p["w_out"][l].astype(BF16),
        "g_ffn": row(p["g_ffn"][l]),
        "w_up_a": chunked(w_up[:, :D_FF]).astype(BF16),
        "w_up_b": chunked(w_up[:, D_FF:]).astype(BF16),
        "ffn_vec": vec,
        "w_down": p["w_down"][l].reshape(N_FF_CHUNKS, FF_CHUNK, D_MODEL).astype(BF16),
    }


def _rope_tables(s_len):
    half = 16
    inv = ROPE_THETA ** (-jnp.arange(half, dtype=F32) * 2.0 / (2 * half))
    pos = jnp.arange(s_len, dtype=F32)
    rows = s_len // GRID_W
    row_pos = jnp.broadcast_to(jnp.arange(rows, dtype=F32)[:, None], (rows, GRID_W)).reshape(s_len)
    col_pos = jnp.broadcast_to(jnp.arange(GRID_W, dtype=F32)[None, :], (rows, GRID_W)).reshape(s_len)

    def cs(p):
        ang = p[:, None] * inv[None, :]
        return jnp.cos(ang), jnp.sin(ang)

    one = lambda n: jnp.ones((s_len, n), F32)
    zero = lambda n: jnp.zeros((s_len, n), F32)
    c, s = cs(pos)
    rope_b = jnp.stack([
        jnp.concatenate([one(64), c, c, one(32)], axis=1),
        jnp.concatenate([zero(64), -s, zero(16), zero(32)], axis=1),
        jnp.concatenate([zero(64), zero(16), s, zero(32)], axis=1)])
    cr, sr = cs(row_pos)
    cc, sc = cs(col_pos)
    z = zero(16)
    rope_c = jnp.stack([
        jnp.tile(jnp.concatenate([cr, cr, cc, cc], axis=1), (1, 2)),
        jnp.tile(jnp.concatenate([-sr, z, -sc, z], axis=1), (1, 2)),
        jnp.tile(jnp.concatenate([z, sr, z, sc], axis=1), (1, 2))])
    return rope_b, rope_c


def _trunk(x, layers, g_final):
    bsz, s_len, _ = x.shape
    nc = s_len // TK
    rope_b, rope_c = _rope_tables(s_len)
    depth = len(layers)
    for l, lw in enumerate(layers):
        qa, ka, va, qb, ktb, vb, qc, ktc, vc, qd, ktd, vd = _proj_call(x, lw, rope_b, rope_c)
        oa = _attn_a_call(qa, ka, va, lw["a_sink"])
        ob = _full_attn_call("b", qb, ktb, vb.reshape(bsz, nc, TK, 512))
        oc = _full_attn_call("c", qc, ktc, vc.reshape(bsz, 2, nc, TK, 128))
        lam_init = 0.8 - 0.6 * math.exp(-0.3 * l)
        od = _full_attn_call("d", qd, ktd, vd.reshape(bsz, nc, TK, 512),
                             extra=(*lw["d_lambda"], lw["d_sub_norm"]), lam_init=lam_init)
        x1, xn = _out_proj_call(x, oa, ob, oc, od, lw)
        x = _ffn_call(xn, x1, lw, g_final, final_norm=(l == depth - 1))
    return x


def kernel(x_prompt, x_sample, g_attn, w_in, a_sink, b_q_norm, b_w_q_up, b_kv_norm, b_w_kv_up,
           c_q_norm, c_k_norm, d_lambda_q1, d_lambda_k1, d_lambda_q2, d_lambda_k2, d_sub_norm, w_out,
           g_ffn, w_up, b_up, conv_w, conv_b, w_down, g_final):
    p = dict(g_attn=g_attn, w_in=w_in, a_sink=a_sink, b_q_norm=b_q_norm, b_w_q_up=b_w_q_up,
             b_kv_norm=b_kv_norm, b_w_kv_up=b_w_kv_up, c_q_norm=c_q_norm, c_k_norm=c_k_norm,
             d_lambda_q1=d_lambda_q1, d_lambda_k1=d_lambda_k1, d_lambda_q2=d_lambda_q2,
             d_lambda_k2=d_lambda_k2, d_sub_norm=d_sub_norm, w_out=w_out, g_ffn=g_ffn, w_up=w_up,
             b_up=b_up, conv_w=conv_w, conv_b=conv_b, w_down=w_down)
    layers = [_pack_layer(p, l) for l in range(g_attn.shape[0])]
    gf = g_final.reshape(1, -1)
    return _trunk(x_prompt, layers, gf), _trunk(x_sample, layers, gf)
```

```python
import functools
import math

import jax
import jax.numpy as jnp
import numpy as np
from jax import lax
from jax.experimental import pallas as pl
from jax.experimental.pallas import tpu as pltpu

F32 = jnp.float32
BF16 = jnp.bfloat16

D_MODEL = 1024
GRID_W = 64
BLOCK = 128
HEAD_DIM = 64
EPS = 1e-6
ROPE_THETA = 10000.0

A_HEADS = 4
WINDOW = 128
B_HEADS = 4
B_Q_RANK = 256
B_KV_RANK = 128
B_NOPE = 64
B_ROPE = 32
B_V = 64
D_QK = 32
D_V = 64
N_ALIBI = 8
D_FF = 2816
IN_SIZES = (256, 128, 128, 256, 128, 32, 256, 128, 128, 256, 256, 256)

SLOPES = tuple(2.0 ** (-8.0 * (i + 1.0) / N_ALIBI) for i in range(N_ALIBI))
LOG2E = math.log2(math.e)

LANES = 128
TM = 512
TK = 512
TQ = 512
TQ_D = 256
TQA = 512
TKA = 256
N_CHUNKS_A = (TQA + 2 * WINDOW) // TKA
SUBLANES = 8
FF_CHUNK = 256
N_FF_CHUNKS = D_FF // FF_CHUNK
VMEM_LIMIT = 56 * 1024 * 1024

_OFF_AQ, _OFF_AK, _OFF_AV = 0, 256, 384
_OFF_BCQ, _OFF_BCKV, _OFF_BKR = 512, 768, 896
_OFF_CQ, _OFF_CK, _OFF_CV = 1024, 1280, 1408
_OFF_DQ, _OFF_DK, _OFF_DV = 1536, 1792, 2048
IN_PACKED = 2304


def _dot(a, b):
    return jnp.dot(a, b, preferred_element_type=F32)


def _rms(x, g):
    return x * lax.rsqrt(jnp.mean(x * x, axis=-1, keepdims=True) + EPS) * g


def _rope(t, tab_ref):
    return (t * tab_ref[0] + pltpu.roll(t, LANES - 16, 1) * tab_ref[1]
            + pltpu.roll(t, 16, 1) * tab_ref[2])


def _dup_halves(t):
    lane = lax.broadcasted_iota(jnp.int32, t.shape, 1)
    sw = pltpu.roll(t, 64, 1)
    lo = lane < 64
    return jnp.where(lo, t, sw), jnp.where(lo, sw, t)


def _with_ones(t):
    lane = lax.broadcasted_iota(jnp.int32, t.shape, 1)
    lo = lane < 64
    return jnp.where(lo, t, 1.0), jnp.where(lo, pltpu.roll(t, 64, 1), 1.0)


def _head_mean_sq(t):
    w = t.shape[1]
    r = lax.broadcasted_iota(jnp.int32, (w, w), 0) // HEAD_DIM
    c = lax.broadcasted_iota(jnp.int32, (w, w), 1) // HEAD_DIM
    pool = jnp.where(r == c, 1.0 / HEAD_DIM, 0.0).astype(BF16)
    t2 = t * t
    hi = t2.astype(BF16)
    lo = (t2 - hi.astype(F32)).astype(BF16)
    return _dot(hi, pool) + _dot(lo, pool)


def _proj_kernel(x_ref, g_ref, win_ref, bqn_ref, wq_ref, bkvn_ref, wkv_ref, cqn_ref, ckn_ref,
                 ropeb_ref, ropec_ref,
                 qa_ref, ka_ref, va_ref, qb_ref, ktb_ref, vb_ref, qc_ref, ktc_ref, vc_ref,
                 qd_ref, ktd_ref, vd_ref):
    x = x_ref[0]
    xn = _rms(x, g_ref[...]).astype(BF16)
    h = _dot(xn, win_ref[...])

    qa_ref[0] = (h[:, _OFF_AQ:_OFF_AQ + 256] * (HEAD_DIM ** -0.5 * LOG2E)).astype(BF16)
    for grp, kd in enumerate(_dup_halves(h[:, _OFF_AK:_OFF_AK + 128])):
        for j in range(TM // BLOCK):
            ka_ref[0, grp, j] = kd[j * BLOCK:(j + 1) * BLOCK].T.astype(BF16)
    v0, v1 = _with_ones(h[:, _OFF_AV:_OFF_AV + 128])
    va_ref[0, 0] = v0.astype(BF16)
    va_ref[0, 1] = v1.astype(BF16)

    cq = _rms(h[:, _OFF_BCQ:_OFF_BCQ + B_Q_RANK], bqn_ref[...]).astype(BF16)
    qb = _dot(cq, wq_ref[...])
    scale_b = (B_NOPE + B_ROPE) ** -0.5 * LOG2E
    for hd in range(B_HEADS):
        t = _rope(qb[:, hd * 128:(hd + 1) * 128], ropeb_ref)
        qb_ref[0, :, hd * 128:(hd + 1) * 128] = (t * scale_b).astype(BF16)
    ckv = _rms(h[:, _OFF_BCKV:_OFF_BCKV + B_KV_RANK], bkvn_ref[...]).astype(BF16)
    kv = _dot(ckv, wkv_ref[...])
    kr = _rope(h[:, _OFF_BKR:_OFF_BKR + 128], ropeb_ref)
    for hd in range(B_HEADS):
        ktb_ref[0, hd, 0] = (kv[:, hd * 128:(hd + 1) * 128] + kr).T.astype(BF16)
    for grp in range(2):
        v0, v1 = _with_ones(kv[:, 512 + grp * 128:512 + (grp + 1) * 128])
        vb_ref[0, :, (2 * grp) * 128:(2 * grp + 1) * 128] = v0.astype(BF16)
        vb_ref[0, :, (2 * grp + 1) * 128:(2 * grp + 2) * 128] = v1.astype(BF16)

    cqh = h[:, _OFF_CQ:_OFF_CQ + 256]
    cqh = cqh * lax.rsqrt(_head_mean_sq(cqh) + EPS) * cqn_ref[...]
    for grp in range(2):
        t = _rope(cqh[:, grp * 128:(grp + 1) * 128], ropec_ref)
        qc_ref[0, :, grp * 128:(grp + 1) * 128] = (t * (HEAD_DIM ** -0.5 * LOG2E)).astype(BF16)
    ckh = h[:, _OFF_CK:_OFF_CK + 128]
    ckh = ckh * lax.rsqrt(_head_mean_sq(ckh) + EPS) * ckn_ref[...]
    ckh = _rope(ckh, ropec_ref)
    k0, k1 = _dup_halves(ckh)
    ktc_ref[0, 0, 0] = k0.T.astype(BF16)
    ktc_ref[0, 1, 0] = k1.T.astype(BF16)
    v0, v1 = _with_ones(h[:, _OFF_CV:_OFF_CV + 128])
    vc_ref[0, 0] = v0.astype(BF16)
    vc_ref[0, 1] = v1.astype(BF16)

    qd_ref[0] = (h[:, _OFF_DQ:_OFF_DQ + 256] * (D_QK ** -0.5 * LOG2E)).astype(BF16)
    for grp in range(2):
        ktd_ref[0, grp, 0] = h[:, _OFF_DK + grp * 128:_OFF_DK + (grp + 1) * 128].T.astype(BF16)
    for grp in range(2):
        v0, v1 = _with_ones(h[:, _OFF_DV + grp * 128:_OFF_DV + (grp + 1) * 128])
        vd_ref[0, :, (2 * grp) * 128:(2 * grp + 1) * 128] = v0.astype(BF16)
        vd_ref[0, :, (2 * grp + 1) * 128:(2 * grp + 2) * 128] = v1.astype(BF16)


def _const_spec(shape):
    nd = len(shape)
    return pl.BlockSpec(shape, lambda *_: (0,) * nd)


def _per_batch_spec(a):
    return pl.BlockSpec((1,) + a.shape[1:], lambda b, i: (b,) + (0,) * (a.ndim - 1))


def _proj_call(x, lw, rope_b, rope_c):
    bsz, s_len, _ = x.shape
    nt = s_len // TM
    tok = lambda w: pl.BlockSpec((1, TM, w), lambda b, i: (b, i, 0))
    dup = pl.BlockSpec((1, 2, TM, 128), lambda b, i: (b, 0, i, 0))
    kt = lambda n: pl.BlockSpec((1, n, 1, 128, TK), lambda b, i: (b, 0, i, 0, 0))
    sds = jax.ShapeDtypeStruct
    out_shape = (
        sds((bsz, s_len, 256), BF16), sds((bsz, 2, s_len // BLOCK, 128, BLOCK), BF16),
        sds((bsz, 2, s_len, 128), BF16),
        sds((bsz, s_len, 512), BF16), sds((bsz, 4, nt, 128, TK), BF16), sds((bsz, s_len, 512), BF16),
        sds((bsz, s_len, 256), BF16), sds((bsz, 2, nt, 128, TK), BF16), sds((bsz, 2, s_len, 128), BF16),
        sds((bsz, s_len, 256), BF16), sds((bsz, 2, nt, 128, TK), BF16), sds((bsz, s_len, 512), BF16),
    )
    kta = pl.BlockSpec((1, 2, TM // BLOCK, 128, BLOCK), lambda b, i: (b, 0, i, 0, 0))
    out_specs = (tok(256), kta, dup, tok(512), kt(4), tok(512),
                 tok(256), kt(2), dup, tok(256), kt(2), tok(512))
    rope_spec = pl.BlockSpec((3, TM, 128), lambda b, i: (0, i, 0))
    in_specs = [
        tok(D_MODEL), _const_spec((1, D_MODEL)), _const_spec((D_MODEL, IN_PACKED)),
        _const_spec((1, B_Q_RANK)), _const_spec((B_Q_RANK, 512)),
        _const_spec((1, B_KV_RANK)), _const_spec((B_KV_RANK, 768)),
        _const_spec((1, 256)), _const_spec((1, 128)),
        rope_spec, rope_spec,
    ]
    return pl.pallas_call(
        _proj_kernel,
        grid=(bsz, nt),
        in_specs=in_specs,
        out_specs=out_specs,
        out_shape=out_shape,
        compiler_params=pltpu.CompilerParams(
            dimension_semantics=("arbitrary", "arbitrary"), vmem_limit_bytes=VMEM_LIMIT),
        name="in_proj",
    )(x, lw["g_attn"], lw["w_in"], lw["b_q_norm"], lw["w_q_up"], lw["b_kv_norm"], lw["w_kv_up"],
      lw["c_q_norm"], lw["c_k_norm"], rope_b, rope_c)


def _fold(op, s):
    out = s[:, 0:LANES]
    for j in range(1, s.shape[1] // LANES):
        out = op(out, s[:, j * LANES:(j + 1) * LANES])
    return out


def _attend(streams, n_chunks, s_scr, add_at=None, m_floor=None):
    n = len(streams)
    rows = streams[0][0].shape[0]
    m_run = [jnp.full((rows, LANES), -jnp.inf, F32)] * n
    acc = [jnp.zeros((rows, LANES), F32)] * n
    m = [None] * n
    for k in range(n + 1):
        for c in range(n_chunks):
            if k < n:
                qm, kt_at, _ = streams[k]
                s = _dot(qm, kt_at(c))
                if add_at is not None:
                    s = s + add_at(k, c)
                s_scr[k % 2, c] = s
                m_run[k] = jnp.maximum(m_run[k], _fold(jnp.maximum, s))
            if k >= 1:
                j = k - 1
                p = jnp.exp2((s_scr[j % 2, c] - m[j]).astype(BF16))
                acc[j] = acc[j] + _dot(p, streams[j][2](c))
        if k < n:
            m[k] = jnp.max(m_run[k], axis=-1, keepdims=True)
            if m_floor is not None:
                m[k] = jnp.maximum(m[k], m_floor[k])
    return acc, m


def _normalised(acc):
    return acc / acc[:, 64:65]


def _pair_lanes(o0, o1):
    lane = lax.broadcasted_iota(jnp.int32, o0.shape, 1)
    return jnp.where(lane < 64, o0, pltpu.roll(o1, 64, 1))


def _half_masks(q):
    lane = lax.broadcasted_iota(jnp.int32, q.shape, 1)
    zero = jnp.zeros_like(q)
    return jnp.where(lane < 64, q, zero), jnp.where(lane < 64, zero, q)


def _attn_b_kernel(q_ref, kt_ref, v_ref, o_ref, s_scr, *, n_chunks):
    streams = [(q_ref[0, :, h * 128:(h + 1) * 128], lambda c, h=h: kt_ref[0, h, c],
                lambda c, h=h: v_ref[0, c, :, h * 128:(h + 1) * 128]) for h in range(B_HEADS)]
    acc, _ = _attend(streams, n_chunks, s_scr)
    for g in range(2):
        o_ref[0, :, g * 128:(g + 1) * 128] = _pair_lanes(
            _normalised(acc[2 * g]), _normalised(acc[2 * g + 1])).astype(BF16)


def _attn_c_kernel(q_ref, kt_ref, v_ref, o_ref, s_scr, *, n_chunks):
    streams = []
    for g in range(2):
        for qm in _half_masks(q_ref[0, :, g * 128:(g + 1) * 128]):
            streams.append((qm, lambda c, g=g: kt_ref[0, g, c], lambda c, g=g: v_ref[0, g, c]))
    acc, _ = _attend(streams, n_chunks, s_scr)
    for g in range(2):
        o_ref[0, :, g * 128:(g + 1) * 128] = _pair_lanes(
            _normalised(acc[2 * g]), _normalised(acc[2 * g + 1])).astype(BF16)


def _attn_d_kernel(q_ref, kt_ref, v_ref, lq1_ref, lk1_ref, lq2_ref, lk2_ref, sn_ref, o_ref, s_scr,
                   bias_scr, *, n_chunks, lam_init):
    grp = pl.program_id(1)
    tq = q_ref.shape[1]
    q0 = pl.program_id(2) * tq
    lam = (jnp.exp(jnp.sum(lq1_ref[...] * lk1_ref[...], axis=-1, keepdims=True))
           - jnp.exp(jnp.sum(lq2_ref[...] * lk2_ref[...], axis=-1, keepdims=True)) + lam_init)
    lane = lax.broadcasted_iota(jnp.int32, (tq, LANES), 1)
    slopes = [(jnp.where(grp == 0, SLOPES[A_HEADS + r], SLOPES[A_HEADS + 2 + r]) * LOG2E).astype(F32)
              for r in range(2)]

    rel = (lax.broadcasted_iota(jnp.int32, (tq, TK), 0)
           - lax.broadcasted_iota(jnp.int32, (tq, TK), 1)).astype(F32)

    def add_at(k, c):
        if k % 2 == 1:
            return bias_scr[c]
        bias = -slopes[k // 2] * jnp.abs(rel + (q0 - c * TK).astype(F32))
        bias_scr[c] = bias
        return bias

    q = q_ref[0]
    streams = [(jnp.where((lane >= D_QK * i) & (lane < D_QK * (i + 1)), q, jnp.zeros_like(q)),
                lambda c: kt_ref[0, 0, c],
                lambda c, r=i // 2: v_ref[0, c, :, r * 128:(r + 1) * 128]) for i in range(4)]
    acc, _ = _attend(streams, n_chunks, s_scr, add_at)
    outs = []
    for r in range(2):
        o = _normalised(acc[2 * r]) - lam * _normalised(acc[2 * r + 1])
        msq = jnp.sum(jnp.where(lane < 64, o * o, 0.0), axis=-1, keepdims=True) * (1.0 / D_V)
        outs.append(o * lax.rsqrt(msq + EPS) * sn_ref[...] * (1.0 - lam_init))
    o_ref[0] = _pair_lanes(outs[0], outs[1]).astype(BF16)


def _full_attn_call(kind, q, kt, v, extra=(), lam_init=0.0):
    bsz, s_len = q.shape[0], q.shape[1]
    tq = TQ_D if kind == "d" else TQ
    nq, nc = s_len // tq, s_len // TK
    tok = lambda w: pl.BlockSpec((1, tq, w), lambda b, i: (b, i, 0))
    scratch = [pltpu.VMEM((2, nc, tq, TK), F32)]
    if kind == "b":
        body = functools.partial(_attn_b_kernel, n_chunks=nc)
        in_specs = [tok(512), _per_batch_spec(kt), _per_batch_spec(v)]
    elif kind == "c":
        body = functools.partial(_attn_c_kernel, n_chunks=nc)
        in_specs = [tok(256), _per_batch_spec(kt), _per_batch_spec(v)]
    else:
        return pl.pallas_call(
            functools.partial(_attn_d_kernel, n_chunks=nc, lam_init=lam_init),
            grid=(bsz, 2, nq),
            in_specs=[pl.BlockSpec((1, tq, 128), lambda b, g, i: (b, i, g)),
                      pl.BlockSpec((1, 1, nc, 128, TK), lambda b, g, i: (b, g, 0, 0, 0)),
                      pl.BlockSpec((1, nc, TK, 256), lambda b, g, i: (b, 0, 0, g)),
                      _const_spec((1, D_QK)), _const_spec((1, D_QK)),
                      _const_spec((1, D_QK)), _const_spec((1, D_QK)),
                      _const_spec((1, 128))],
            out_specs=pl.BlockSpec((1, tq, 128), lambda b, g, i: (b, i, g)),
            out_shape=jax.ShapeDtypeStruct((bsz, s_len, 256), BF16),
            scratch_shapes=scratch + [pltpu.VMEM((nc, tq, TK), F32)],
            compiler_params=pltpu.CompilerParams(
                dimension_semantics=("arbitrary", "arbitrary", "arbitrary"),
                vmem_limit_bytes=VMEM_LIMIT),
            name="attn_d",
        )(q, kt, v, *extra)
    return pl.pallas_call(
        body,
        grid=(bsz, nq),
        in_specs=in_specs,
        out_specs=tok(256),
        out_shape=jax.ShapeDtypeStruct((bsz, s_len, 256), BF16),
        scratch_shapes=scratch,
        compiler_params=pltpu.CompilerParams(
            dimension_semantics=("arbitrary", "arbitrary"), vmem_limit_bytes=VMEM_LIMIT),
        name="attn_" + kind,
    )(q, kt, v, *extra)


def _attn_a_kernel(q_ref, kt_ref, v_ref, sink_ref, o_ref, s_scr, *, s_len):
    tile = pl.program_id(1)
    q0 = tile * TQA
    n_blocks = s_len // BLOCK
    per_chunk = TKA // BLOCK
    first = tile * (TQA // BLOCK) - 1

    def key_blocks(c):
        return [jnp.clip(first + per_chunk * c + j, 0, n_blocks - 1) for j in range(per_chunk)]

    def kt_at(g, c):
        return jnp.concatenate([kt_ref[0, g, i] for i in key_blocks(c)], axis=1)

    def v_at(g, c):
        return jnp.concatenate(
            [v_ref[0, g, pl.ds(pl.multiple_of(i * BLOCK, BLOCK), BLOCK), :] for i in key_blocks(c)], axis=0)

    neg_dist = []
    for c in range(N_CHUNKS_A):
        col = lax.broadcasted_iota(jnp.int32, (TQA, TKA), 1)
        diff = lax.broadcasted_iota(jnp.int32, (TQA, TKA), 0) - col + (BLOCK - TKA * c)
        kpos = q0 - BLOCK + TKA * c + col
        valid = (jnp.abs(diff) <= WINDOW) & (kpos >= 0) & (kpos < s_len)
        neg_dist.append(jnp.where(valid, -jnp.abs(diff).astype(F32), -jnp.inf))

    streams = []
    for g in range(2):
        for qm in _half_masks(q_ref[0, :, g * 128:(g + 1) * 128]):
            streams.append((qm, functools.partial(kt_at, g), functools.partial(v_at, g)))
    sinks = [sink_ref[:, h:h + 1] * LOG2E for h in range(A_HEADS)]
    acc, m = _attend(streams, N_CHUNKS_A, s_scr,
                     add_at=lambda k, c: (SLOPES[k] * LOG2E) * neg_dist[c], m_floor=sinks)
    outs = []
    for h in range(A_HEADS):
        den = acc[h] + jnp.exp2(sinks[h] - m[h])
        outs.append(acc[h] / den[:, 64:65])
    for g in range(2):
        o_ref[0, :, g * 128:(g + 1) * 128] = _pair_lanes(outs[2 * g], outs[2 * g + 1]).astype(BF16)


def _attn_a_call(q, kt, v, sink):
    bsz, s_len = q.shape[0], q.shape[1]
    tok = pl.BlockSpec((1, TQA, 256), lambda b, i: (b, i, 0))
    return pl.pallas_call(
        functools.partial(_attn_a_kernel, s_len=s_len),
        grid=(bsz, s_len // TQA),
        in_specs=[tok, _per_batch_spec(kt), _per_batch_spec(v), _const_spec((1, A_HEADS))],
        out_specs=tok,
        out_shape=jax.ShapeDtypeStruct((bsz, s_len, 256), BF16),
        scratch_shapes=[pltpu.VMEM((2, N_CHUNKS_A, TQA, TKA), F32)],
        compiler_params=pltpu.CompilerParams(
            dimension_semantics=("arbitrary", "arbitrary"), vmem_limit_bytes=VMEM_LIMIT),
        name="attn_a",
    )(q, kt, v, sink)


def _out_proj_kernel(x_ref, oa_ref, ob_ref, oc_ref, od_ref, wo_ref, x1_ref):
    x1 = x_ref[0]
    for j, o_ref in enumerate((oa_ref, ob_ref, oc_ref, od_ref)):
        x1 = x1 + _dot(o_ref[0], wo_ref[j * 256:(j + 1) * 256, :])
    x1_ref[0] = x1


def _out_proj_call(x, oa, ob, oc, od, lw):
    bsz, s_len, _ = x.shape
    tok = lambda w: pl.BlockSpec((1, TM, w), lambda b, i: (b, i, 0))
    return pl.pallas_call(
        _out_proj_kernel,
        grid=(bsz, s_len // TM),
        in_specs=[tok(D_MODEL), tok(256), tok(256), tok(256), tok(256),
                  _const_spec((D_MODEL, D_MODEL))],
        out_specs=tok(D_MODEL),
        out_shape=jax.ShapeDtypeStruct((bsz, s_len, D_MODEL), F32),
        compiler_params=pltpu.CompilerParams(
            dimension_semantics=("arbitrary", "arbitrary"), vmem_limit_bytes=VMEM_LIMIT),
        name="out_proj",
    )(x, oa, ob, oc, od, lw["w_out"])


def _transpose8(blocks):
    sub = lax.broadcasted_iota(jnp.int32, blocks[0].shape, 0)
    for d in (4, 2, 1):
        keep = (sub & d) == 0
        nxt = list(blocks)
        for i in range(SUBLANES):
            if i & d == 0:
                lo, hi = blocks[i], blocks[i + d]
                nxt[i] = jnp.where(keep, lo, pltpu.roll(hi, d, 0))
                nxt[i + d] = jnp.where(keep, pltpu.roll(lo, SUBLANES - d, 0), hi)
        blocks = nxt
    return blocks


def _ffn_kernel(x1_ref, prev_ref, next_ref, g_ref, wua_ref, wub_ref, vec_ref, wd_ref, gf_ref,
                o_ref, xf_ref, xb_ref, h0_ref, h1_ref, act_ref, *, final_norm):
    first = pl.program_id(1) == 0
    last = pl.program_id(1) == pl.num_programs(1) - 1
    nv = TM // SUBLANES

    sub = lax.broadcasted_iota(jnp.int32, (SUBLANES, D_MODEL), 0)
    for jb in range(nv // SUBLANES):
        blocks = [x1_ref[0, s * nv + jb * SUBLANES:s * nv + (jb + 1) * SUBLANES, :] for s in range(SUBLANES)]
        for jj, rows in enumerate(_transpose8(blocks)):
            j = jb * SUBLANES + jj
            xf_ref[SUBLANES * (j + 1):SUBLANES * (j + 2), :] = rows
            if j == nv - 1:
                xf_ref[0:SUBLANES, :] = jnp.where(
                    sub == 0, prev_ref[0, SUBLANES - 1:SUBLANES, :], pltpu.roll(rows, 1, 0))
            if j == 0:
                xf_ref[TM + SUBLANES:TM + 2 * SUBLANES, :] = jnp.where(
                    sub == SUBLANES - 1, next_ref[0, 0:1, :], pltpu.roll(rows, SUBLANES - 1, 0))
    xb_ref[...] = _rms(xf_ref[...], g_ref[...]).astype(BF16)

    def up(c, h_ref):
        xe = xb_ref[...]
        h_ref[0] = _dot(xe, wua_ref[c])
        h_ref[1] = _dot(xe, wub_ref[c])

    sub_h = lax.broadcasted_iota(jnp.int32, (SUBLANES, FF_CHUNK), 0)

    def conv(h_ref, half, bias, w0, w1, w2, cb):
        top = h_ref[half, 0:SUBLANES, :]
        h_ref[half, 0:SUBLANES, :] = jnp.where(first & (sub_h == 0), -bias, top)
        bot = h_ref[half, TM + SUBLANES:TM + 2 * SUBLANES, :]
        h_ref[half, TM + SUBLANES:TM + 2 * SUBLANES, :] = jnp.where(last & (sub_h == SUBLANES - 1), -bias, bot)
        return (w0 * h_ref[half, 0:TM, :] + w1 * h_ref[half, SUBLANES:SUBLANES + TM, :]
                + w2 * h_ref[half, 2 * SUBLANES:2 * SUBLANES + TM, :] + (bias * (w0 + w1 + w2) + cb))

    def gate(c, h_ref):
        vec = vec_ref[c]
        ca = conv(h_ref, 0, vec[0:1], vec[2:3], vec[3:4], vec[4:5], vec[8:9])
        cb = conv(h_ref, 1, vec[1:2], vec[5:6], vec[6:7], vec[7:8], vec[9:10])
        act_ref[c] = ((ca / (1.0 + jnp.exp(-ca))) * cb).astype(BF16)

    up(0, h0_ref)

    def pair(j, carry):
        up(2 * j + 1, h1_ref)
        gate(2 * j, h0_ref)
        up(2 * j + 2, h0_ref)
        gate(2 * j + 1, h1_ref)
        return carry

    lax.fori_loop(0, (N_FF_CHUNKS - 1) // 2, pair, 0)
    gate(N_FF_CHUNKS - 1, h0_ref)

    y = _dot(act_ref[0], wd_ref[0])
    for c in range(1, N_FF_CHUNKS):
        y = y + _dot(act_ref[c], wd_ref[c])
    xf_ref[0:TM, :] = y
    for jb in range(nv // SUBLANES):
        groups = [xf_ref[(jb * SUBLANES + jj) * SUBLANES:(jb * SUBLANES + jj + 1) * SUBLANES, :]
                  for jj in range(SUBLANES)]
        for s, blk in enumerate(_transpose8(groups)):
            rows = slice(s * nv + jb * SUBLANES, s * nv + (jb + 1) * SUBLANES)
            out = x1_ref[0, rows, :] + blk
            if final_norm:
                out = _rms(out, gf_ref[...])
            o_ref[0, rows, :] = out


def _ffn_call(x1, lw, g_final, final_norm):
    bsz, s_len, _ = x1.shape
    per_tile = TM // SUBLANES
    last = s_len // SUBLANES - 1
    tok = pl.BlockSpec((1, TM, D_MODEL), lambda b, i: (b, i, 0))
    prev = pl.BlockSpec((1, SUBLANES, D_MODEL), lambda b, i: (b, jnp.maximum(i * per_tile - 1, 0), 0))
    nxt = pl.BlockSpec((1, SUBLANES, D_MODEL),
                       lambda b, i: (b, jnp.minimum((i + 1) * per_tile, last), 0))
    rows = TM + 2 * SUBLANES
    return pl.pallas_call(
        functools.partial(_ffn_kernel, final_norm=final_norm),
        grid=(bsz, s_len // TM),
        in_specs=[tok, prev, nxt, _const_spec((1, D_MODEL)),
                  _const_spec((N_FF_CHUNKS, D_MODEL, FF_CHUNK)),
                  _const_spec((N_FF_CHUNKS, D_MODEL, FF_CHUNK)),
                  _const_spec((N_FF_CHUNKS, 16, FF_CHUNK)),
                  _const_spec((N_FF_CHUNKS, FF_CHUNK, D_MODEL)),
                  _const_spec((1, D_MODEL))],
        out_specs=tok,
        out_shape=jax.ShapeDtypeStruct((bsz, s_len, D_MODEL), F32),
        scratch_shapes=[pltpu.VMEM((rows, D_MODEL), F32),
                        pltpu.VMEM((rows, D_MODEL), BF16),
                        pltpu.VMEM((2, rows, FF_CHUNK), F32),
                        pltpu.VMEM((2, rows, FF_CHUNK), F32),
                        pltpu.VMEM((N_FF_CHUNKS, TM, FF_CHUNK), BF16)],
        compiler_params=pltpu.CompilerParams(
            dimension_semantics=("arbitrary", "arbitrary"), vmem_limit_bytes=VMEM_LIMIT),
        name="ffn",
    )(x1, x1, x1, lw["g_ffn"], lw["w_up_a"], lw["w_up_b"], lw["ffn_vec"], lw["w_down"], g_final)


def _pack_layer(p, l):
    pieces = jnp.split(p["w_in"][l], np.cumsum(IN_SIZES)[:-1].tolist(), axis=1)
    aq, ak, av, bcq, bckv, bkr, cq, ck, cv, dq, dk, dv = pieces
    z = lambda n: jnp.zeros((D_MODEL, n), F32)
    w_in = jnp.concatenate([aq, ak, av, bcq, bckv, z(64), bkr, z(32), cq, ck, cv, dq, dk, dv], axis=1)

    wq = p["b_w_q_up"][l].reshape(B_Q_RANK, B_HEADS, B_NOPE + B_ROPE)
    wq = jnp.pad(wq, ((0, 0), (0, 0), (0, 128 - B_NOPE - B_ROPE))).reshape(B_Q_RANK, B_HEADS * 128)
    wkv = p["b_w_kv_up"][l].reshape(B_KV_RANK, B_HEADS, B_NOPE + B_V)
    wk = jnp.pad(wkv[:, :, :B_NOPE], ((0, 0), (0, 0), (0, 128 - B_NOPE))).reshape(B_KV_RANK, B_HEADS * 128)
    wv = wkv[:, :, B_NOPE:].reshape(B_KV_RANK, B_HEADS * B_V)

    w_up = p["w_up"][l]
    chunked = lambda w: w.reshape(D_MODEL, N_FF_CHUNKS, FF_CHUNK).transpose(1, 0, 2)
    halves = lambda v: (v[..., :D_FF], v[..., D_FF:])
    bua, bub = halves(p["b_up"][l])
    cwa, cwb = halves(p["conv_w"][l])
    cba, cbb = halves(p["conv_b"][l])
    rows = [bua, bub, cwa[0], cwa[1], cwa[2], cwb[0], cwb[1], cwb[2], cba, cbb]
    vec = jnp.stack(rows + [jnp.zeros_like(bua)] * (16 - len(rows)), axis=0)
    vec = vec.reshape(16, N_FF_CHUNKS, FF_CHUNK).transpose(1, 0, 2)
    row = lambda v: v.reshape(1, -1)
    return {
        "g_attn": row(p["g_attn"][l]),
        "w_in": w_in.astype(BF16),
        "a_sink": row(p["a_sink"][l]),
        "b_q_norm": row(p["b_q_norm"][l]),
        "w_q_up": wq.astype(BF16),
        "b_kv_norm": row(p["b_kv_norm"][l]),
        "w_kv_up": jnp.concatenate([wk, wv], axis=1).astype(BF16),
        "c_q_norm": row(jnp.tile(p["c_q_norm"][l], 4)),
        "c_k_norm": row(jnp.tile(p["c_k_norm"][l], 2)),
        "d_lambda": tuple(row(p[n][l]) for n in ("d_lambda_q1", "d_lambda_k1", "d_lambda_q2", "d_lambda_k2")),
        "d_sub_norm": row(jnp.tile(p["d_sub_norm"][l], 2)),
        "w_out": p["w_out"][l].astype(BF16),
        "g_ffn": row(p["g_ffn"][l]),
        "w_up_a": chunked(w_up[:, :D_FF]).astype(BF16),
        "w_up_b": chunked(w_up[:, D_FF:]).astype(BF16),
        "ffn_vec": vec,
        "w_down": p["w_down"][l].reshape(N_FF_CHUNKS, FF_CHUNK, D_MODEL).astype(BF16),
    }


def _rope_tables(s_len):
    half = 16
    inv = ROPE_THETA ** (-jnp.arange(half, dtype=F32) * 2.0 / (2 * half))
    pos = jnp.arange(s_len, dtype=F32)
    rows = s_len // GRID_W
    row_pos = jnp.broadcast_to(jnp.arange(rows, dtype=F32)[:, None], (rows, GRID_W)).reshape(s_len)
    col_pos = jnp.broadcast_to(jnp.arange(GRID_W, dtype=F32)[None, :], (rows, GRID_W)).reshape(s_len)

    def cs(p):
        ang = p[:, None] * inv[None, :]
        return jnp.cos(ang), jnp.sin(ang)

    one = lambda n: jnp.ones((s_len, n), F32)
    zero = lambda n: jnp.zeros((s_len, n), F32)
    c, s = cs(pos)
    rope_b = jnp.stack([
        jnp.concatenate([one(64), c, c, one(32)], axis=1),
        jnp.concatenate([zero(64), -s, zero(16), zero(32)], axis=1),
        jnp.concatenate([zero(64), zero(16), s, zero(32)], axis=1)])
    cr, sr = cs(row_pos)
    cc, sc = cs(col_pos)
    z = zero(16)
    rope_c = jnp.stack([
        jnp.tile(jnp.concatenate([cr, cr, cc, cc], axis=1), (1, 2)),
        jnp.tile(jnp.concatenate([-sr, z, -sc, z], axis=1), (1, 2)),
        jnp.tile(jnp.concatenate([z, sr, z, sc], axis=1), (1, 2))])
    return rope_b, rope_c


def _trunk(x, layers, g_final):
    bsz, s_len, _ = x.shape
    nc = s_len // TK
    rope_b, rope_c = _rope_tables(s_len)
    depth = len(layers)
    for l, lw in enumerate(layers):
        qa, ka, va, qb, ktb, vb, qc, ktc, vc, qd, ktd, vd = _proj_call(x, lw, rope_b, rope_c)
        oa = _attn_a_call(qa, ka, va, lw["a_sink"])
        ob = _full_attn_call("b", qb, ktb, vb.reshape(bsz, nc, TK, 512))
        oc = _full_attn_call("c", qc, ktc, vc.reshape(bsz, 2, nc, TK, 128))
        lam_init = 0.8 - 0.6 * math.exp(-0.3 * l)
        od = _full_attn_call("d", qd, ktd, vd.reshape(bsz, nc, TK, 512),
                             extra=(*lw["d_lambda"], lw["d_sub_norm"]), lam_init=lam_init)
        x1 = _out_proj_call(x, oa, ob, oc, od, lw)
        x = _ffn_call(x1, lw, g_final, final_norm=(l == depth - 1))
    return x


def kernel(x_prompt, x_sample, g_attn, w_in, a_sink, b_q_norm, b_w_q_up, b_kv_norm, b_w_kv_up,
           c_q_norm, c_k_norm, d_lambda_q1, d_lambda_k1, d_lambda_q2, d_lambda_k2, d_sub_norm, w_out,
           g_ffn, w_up, b_up, conv_w, conv_b, w_down, g_final):
    p = dict(g_attn=g_attn, w_in=w_in, a_sink=a_sink, b_q_norm=b_q_norm, b_w_q_up=b_w_q_up,
             b_kv_norm=b_kv_norm, b_w_kv_up=b_w_kv_up, c_q_norm=c_q_norm, c_k_norm=c_k_norm,
             d_lambda_q1=d_lambda_q1, d_lambda_k1=d_lambda_k1, d_lambda_q2=d_lambda_q2,
             d_lambda_k2=d_lambda_k2, d_sub_norm=d_sub_norm, w_out=w_out, g_ffn=g_ffn, w_up=w_up,
             b_up=b_up, conv_w=conv_w, conv_b=conv_b, w_down=w_down)
    layers = [_pack_layer(p, l) for l in range(g_attn.shape[0])]
    gf = g_final.reshape(1, -1)
    return _trunk(x_prompt, layers, gf), _trunk(x_sample, layers, gf)
```

```python
import functools
import math

import jax
import jax.numpy as jnp
import numpy as np
from jax import lax
from jax.experimental import pallas as pl
from jax.experimental.pallas import tpu as pltpu

F32 = jnp.float32
BF16 = jnp.bfloat16

D_MODEL = 1024
GRID_W = 64
BLOCK = 128
HEAD_DIM = 64
EPS = 1e-6
ROPE_THETA = 10000.0

A_HEADS = 4
WINDOW = 128
B_HEADS = 4
B_Q_RANK = 256
B_KV_RANK = 128
B_NOPE = 64
B_ROPE = 32
B_V = 64
D_QK = 32
D_V = 64
N_ALIBI = 8
D_FF = 2816
IN_SIZES = (256, 128, 128, 256, 128, 32, 256, 128, 128, 256, 256, 256)

SLOPES = tuple(2.0 ** (-8.0 * (i + 1.0) / N_ALIBI) for i in range(N_ALIBI))
LOG2E = math.log2(math.e)


def _split3(x):
    rest, out = np.float32(x), []
    for _ in range(3):
        part = np.float32(np.asarray(rest).astype(jnp.bfloat16))
        out.append(float(part))
        rest = np.float32(rest - part)
    return tuple(out)


D_SLOPE_PARTS = tuple(_split3(SLOPES[A_HEADS + h] * LOG2E) for h in range(4))

LANES = 128
TM = 512
TK = 512
TQ = 512
TQ_D = 256
TQA = 512
TKA = 256
N_CHUNKS_A = (TQA + 2 * WINDOW) // TKA
SUBLANES = 8
FF_CHUNK = 256
N_FF_CHUNKS = D_FF // FF_CHUNK
VMEM_LIMIT = 56 * 1024 * 1024

_OFF_AQ, _OFF_AK, _OFF_AV = 0, 256, 384
_OFF_BCQ, _OFF_BCKV, _OFF_BKR = 512, 768, 896
_OFF_CQ, _OFF_CK, _OFF_CV = 1024, 1280, 1408
_OFF_DQ, _OFF_DK, _OFF_DV = 1536, 1792, 2048
IN_PACKED = 2304


def _dot(a, b):
    return jnp.dot(a, b, preferred_element_type=F32)


def _rms(x, g):
    return x * lax.rsqrt(jnp.mean(x * x, axis=-1, keepdims=True) + EPS) * g


def _rope(t, tab_ref):
    return (t * tab_ref[0] + pltpu.roll(t, LANES - 16, 1) * tab_ref[1]
            + pltpu.roll(t, 16, 1) * tab_ref[2])


def _dup_halves(t):
    lane = lax.broadcasted_iota(jnp.int32, t.shape, 1)
    sw = pltpu.roll(t, 64, 1)
    lo = lane < 64
    return jnp.where(lo, t, sw), jnp.where(lo, sw, t)


def _with_ones(t):
    lane = lax.broadcasted_iota(jnp.int32, t.shape, 1)
    lo = lane < 64
    return jnp.where(lo, t, 1.0), jnp.where(lo, pltpu.roll(t, 64, 1), 1.0)


def _head_mean_sq(t):
    w = t.shape[1]
    r = lax.broadcasted_iota(jnp.int32, (w, w), 0) // HEAD_DIM
    c = lax.broadcasted_iota(jnp.int32, (w, w), 1) // HEAD_DIM
    pool = jnp.where(r == c, 1.0 / HEAD_DIM, 0.0).astype(BF16)
    t2 = t * t
    hi = t2.astype(BF16)
    lo = (t2 - hi.astype(F32)).astype(BF16)
    return _dot(hi, pool) + _dot(lo, pool)


def _proj_kernel(x_ref, g_ref, win_ref, bqn_ref, wq_ref, bkvn_ref, wkv_ref, cqn_ref, ckn_ref,
                 ropeb_ref, ropec_ref,
                 qa_ref, ka_ref, va_ref, qb_ref, ktb_ref, vb_ref, qc_ref, ktc_ref, vc_ref,
                 qd_ref, ktd_ref, vd_ref):
    x = x_ref[0]
    xn = _rms(x, g_ref[...]).astype(BF16)
    h = _dot(xn, win_ref[...])

    qa_ref[0] = (h[:, _OFF_AQ:_OFF_AQ + 256] * (HEAD_DIM ** -0.5 * LOG2E)).astype(BF16)
    for grp, kd in enumerate(_dup_halves(h[:, _OFF_AK:_OFF_AK + 128])):
        for j in range(TM // BLOCK):
            ka_ref[0, grp, j] = kd[j * BLOCK:(j + 1) * BLOCK].T.astype(BF16)
    v0, v1 = _with_ones(h[:, _OFF_AV:_OFF_AV + 128])
    va_ref[0, 0] = v0.astype(BF16)
    va_ref[0, 1] = v1.astype(BF16)

    cq = _rms(h[:, _OFF_BCQ:_OFF_BCQ + B_Q_RANK], bqn_ref[...]).astype(BF16)
    qb = _dot(cq, wq_ref[...])
    scale_b = (B_NOPE + B_ROPE) ** -0.5 * LOG2E
    for hd in range(B_HEADS):
        t = _rope(qb[:, hd * 128:(hd + 1) * 128], ropeb_ref)
        qb_ref[0, :, hd * 128:(hd + 1) * 128] = (t * scale_b).astype(BF16)
    ckv = _rms(h[:, _OFF_BCKV:_OFF_BCKV + B_KV_RANK], bkvn_ref[...]).astype(BF16)
    kv = _dot(ckv, wkv_ref[...])
    kr = _rope(h[:, _OFF_BKR:_OFF_BKR + 128], ropeb_ref)
    for hd in range(B_HEADS):
        ktb_ref[0, hd, 0] = (kv[:, hd * 128:(hd + 1) * 128] + kr).T.astype(BF16)
    for grp in range(2):
        v0, v1 = _with_ones(kv[:, 512 + grp * 128:512 + (grp + 1) * 128])
        vb_ref[0, :, (2 * grp) * 128:(2 * grp + 1) * 128] = v0.astype(BF16)
        vb_ref[0, :, (2 * grp + 1) * 128:(2 * grp + 2) * 128] = v1.astype(BF16)

    cqh = h[:, _OFF_CQ:_OFF_CQ + 256]
    cqh = cqh * lax.rsqrt(_head_mean_sq(cqh) + EPS) * cqn_ref[...]
    for grp in range(2):
        t = _rope(cqh[:, grp * 128:(grp + 1) * 128], ropec_ref)
        qc_ref[0, :, grp * 128:(grp + 1) * 128] = (t * (HEAD_DIM ** -0.5 * LOG2E)).astype(BF16)
    ckh = h[:, _OFF_CK:_OFF_CK + 128]
    ckh = ckh * lax.rsqrt(_head_mean_sq(ckh) + EPS) * ckn_ref[...]
    ckh = _rope(ckh, ropec_ref)
    k0, k1 = _dup_halves(ckh)
    ktc_ref[0, 0, 0] = k0.T.astype(BF16)
    ktc_ref[0, 1, 0] = k1.T.astype(BF16)
    v0, v1 = _with_ones(h[:, _OFF_CV:_OFF_CV + 128])
    vc_ref[0, 0] = v0.astype(BF16)
    vc_ref[0, 1] = v1.astype(BF16)

    lane = lax.broadcasted_iota(jnp.int32, (TM, LANES), 1)
    for grp in range(2):
        qg = h[:, _OFF_DQ + grp * 128:_OFF_DQ + (grp + 1) * 128] * (D_QK ** -0.5 * LOG2E)
        for u in range(4):
            t = qg if u == 0 else pltpu.roll(qg, LANES - D_QK * u, 1)
            qd_ref[0, :, (4 * grp + u) * LANES:(4 * grp + u + 1) * LANES] = (
                jnp.where(lane < D_QK, t, 0.0).astype(BF16))
    for grp in range(2):
        ktd_ref[0, grp, 0] = h[:, _OFF_DK + grp * 128:_OFF_DK + (grp + 1) * 128].T.astype(BF16)
    for grp in range(2):
        v0, v1 = _with_ones(h[:, _OFF_DV + grp * 128:_OFF_DV + (grp + 1) * 128])
        vd_ref[0, :, (2 * grp) * 128:(2 * grp + 1) * 128] = v0.astype(BF16)
        vd_ref[0, :, (2 * grp + 1) * 128:(2 * grp + 2) * 128] = v1.astype(BF16)


def _const_spec(shape):
    nd = len(shape)
    return pl.BlockSpec(shape, lambda *_: (0,) * nd)


def _per_batch_spec(a):
    return pl.BlockSpec((1,) + a.shape[1:], lambda b, i: (b,) + (0,) * (a.ndim - 1))


def _proj_call(x, lw, rope_b, rope_c):
    bsz, s_len, _ = x.shape
    nt = s_len // TM
    tok = lambda w: pl.BlockSpec((1, TM, w), lambda b, i: (b, i, 0))
    dup = pl.BlockSpec((1, 2, TM, 128), lambda b, i: (b, 0, i, 0))
    kt = lambda n: pl.BlockSpec((1, n, 1, 128, TK), lambda b, i: (b, 0, i, 0, 0))
    sds = jax.ShapeDtypeStruct
    out_shape = (
        sds((bsz, s_len, 256), BF16), sds((bsz, 2, s_len // BLOCK, 128, BLOCK), BF16),
        sds((bsz, 2, s_len, 128), BF16),
        sds((bsz, s_len, 512), BF16), sds((bsz, 4, nt, 128, TK), BF16), sds((bsz, s_len, 512), BF16),
        sds((bsz, s_len, 256), BF16), sds((bsz, 2, nt, 128, TK), BF16), sds((bsz, 2, s_len, 128), BF16),
        sds((bsz, s_len, 1024), BF16), sds((bsz, 2, nt, 128, TK), BF16), sds((bsz, s_len, 512), BF16),
    )
    kta = pl.BlockSpec((1, 2, TM // BLOCK, 128, BLOCK), lambda b, i: (b, 0, i, 0, 0))
    out_specs = (tok(256), kta, dup, tok(512), kt(4), tok(512),
                 tok(256), kt(2), dup, tok(1024), kt(2), tok(512))
    rope_spec = pl.BlockSpec((3, TM, 128), lambda b, i: (0, i, 0))
    in_specs = [
        tok(D_MODEL), _const_spec((1, D_MODEL)), _const_spec((D_MODEL, IN_PACKED)),
        _const_spec((1, B_Q_RANK)), _const_spec((B_Q_RANK, 512)),
        _const_spec((1, B_KV_RANK)), _const_spec((B_KV_RANK, 768)),
        _const_spec((1, 256)), _const_spec((1, 128)),
        rope_spec, rope_spec,
    ]
    return pl.pallas_call(
        _proj_kernel,
        grid=(bsz, nt),
        in_specs=in_specs,
        out_specs=out_specs,
        out_shape=out_shape,
        compiler_params=pltpu.CompilerParams(
            dimension_semantics=("arbitrary", "arbitrary"), vmem_limit_bytes=VMEM_LIMIT),
        name="in_proj",
    )(x, lw["g_attn"], lw["w_in"], lw["b_q_norm"], lw["w_q_up"], lw["b_kv_norm"], lw["w_kv_up"],
      lw["c_q_norm"], lw["c_k_norm"], rope_b, rope_c)


def _fold(op, s):
    out = s[:, 0:LANES]
    for j in range(1, s.shape[1] // LANES):
        out = op(out, s[:, j * LANES:(j + 1) * LANES])
    return out


def _attend(streams, n_chunks, s_scr, add_at=None, m_floor=None):
    n = len(streams)
    rows = s_scr.shape[2]
    m_run = [jnp.full((rows, LANES), -jnp.inf, F32)] * n
    acc = [jnp.zeros((rows, LANES), F32)] * n
    m = [None] * n
    for k in range(n + 1):
        for c in range(n_chunks):
            if k < n:
                qm, kt_at, _ = streams[k]
                s = _dot(qm(c) if callable(qm) else qm, kt_at(c))
                bias = None if add_at is None else add_at(k, c)
                if bias is not None:
                    s = s + bias
                s_scr[k % 2, c] = s
                m_run[k] = jnp.maximum(m_run[k], _fold(jnp.maximum, s))
            if k >= 1:
                j = k - 1
                p = jnp.exp2((s_scr[j % 2, c] - m[j]).astype(BF16))
                acc[j] = acc[j] + _dot(p, streams[j][2](c))
        if k < n:
            m[k] = jnp.max(m_run[k], axis=-1, keepdims=True)
            if m_floor is not None:
                m[k] = jnp.maximum(m[k], m_floor[k])
    return acc, m


def _normalised(acc):
    return acc / acc[:, 64:65]


def _pair_lanes(o0, o1):
    lane = lax.broadcasted_iota(jnp.int32, o0.shape, 1)
    return jnp.where(lane < 64, o0, pltpu.roll(o1, 64, 1))


def _half_masks(q):
    lane = lax.broadcasted_iota(jnp.int32, q.shape, 1)
    zero = jnp.zeros_like(q)
    return jnp.where(lane < 64, q, zero), jnp.where(lane < 64, zero, q)


def _attn_b_kernel(q_ref, kt_ref, v_ref, o_ref, s_scr, *, n_chunks):
    streams = [(q_ref[0, :, h * 128:(h + 1) * 128], lambda c, h=h: kt_ref[0, h, c],
                lambda c, h=h: v_ref[0, c, :, h * 128:(h + 1) * 128]) for h in range(B_HEADS)]
    acc, _ = _attend(streams, n_chunks, s_scr)
    for g in range(2):
        o_ref[0, :, g * 128:(g + 1) * 128] = _pair_lanes(
            _normalised(acc[2 * g]), _normalised(acc[2 * g + 1])).astype(BF16)


def _attn_c_kernel(q_ref, kt_ref, v_ref, o_ref, s_scr, *, n_chunks):
    streams = []
    for g in range(2):
        for qm in _half_masks(q_ref[0, :, g * 128:(g + 1) * 128]):
            streams.append((qm, lambda c, g=g: kt_ref[0, g, c], lambda c, g=g: v_ref[0, g, c]))
    acc, _ = _attend(streams, n_chunks, s_scr)
    for g in range(2):
        o_ref[0, :, g * 128:(g + 1) * 128] = _pair_lanes(
            _normalised(acc[2 * g]), _normalised(acc[2 * g + 1])).astype(BF16)


def _attn_d_kernel(q_ref, kt_ref, v_ref, lq1_ref, lk1_ref, lq2_ref, lk2_ref, sn_ref, o_ref, s_scr,
                   bias_scr, *, n_chunks, lam_init):
    grp = pl.program_id(1)
    tq = q_ref.shape[1]
    q0 = pl.program_id(2) * tq
    diag = q0 // TK
    lam = (jnp.exp(jnp.sum(lq1_ref[...] * lk1_ref[...], axis=-1, keepdims=True))
           - jnp.exp(jnp.sum(lq2_ref[...] * lk2_ref[...], axis=-1, keepdims=True)) + lam_init)
    lane = lax.broadcasted_iota(jnp.int32, (tq, LANES), 1)
    row = lax.broadcasted_iota(jnp.int32, (tq, LANES), 0).astype(F32)
    parts = [[jnp.where(grp == 0, D_SLOPE_PARTS[r][p], D_SLOPE_PARTS[2 + r][p]).astype(F32)
              for p in range(3)] for r in range(2)]

    f = lane - D_QK
    third = lambda idx, p: jnp.where(idx % 3 == 0, p[0], jnp.where(idx % 3 == 1, p[1], p[2]))
    q_feat, k_feat, sigma = [], [], []
    kr = lax.broadcasted_iota(jnp.int32, (LANES - D_QK, TK), 0)
    kc = lax.broadcasted_iota(jnp.int32, (LANES - D_QK, TK), 1)
    for r in range(2):
        left = jnp.where((f >= 0) & (f < 6), third(f, parts[r]), jnp.where((f >= 6) & (f < 9), -row, 0.0))
        q_feat.append((left.astype(BF16), (-left).astype(BF16)))
        sig_rows = third(kr, parts[r])
        k_feat.append(jnp.where(kr < 3, (kc % 256).astype(F32),
                      jnp.where(kr < 6, (kc // 256 * 256).astype(F32),
                      jnp.where(kr < 9, sig_rows,
                      jnp.where(kr < 12, 256.0 * sig_rows, 0.0)))).astype(BF16))
        sigma.append(parts[r][0] + parts[r][1] + parts[r][2])
    off_lanes = (f >= 9) & (f < 12)

    def chunk_of(d):
        wrapped = diag + d >= n_chunks
        return jnp.where(wrapped, diag + d - n_chunks, diag + d), wrapped

    def q_at(u, d):
        qm = q_ref[0, :, u * LANES:(u + 1) * LANES]
        if d == 0:
            return qm
        c, is_left = chunk_of(d)
        off = ((q0 - c * TK) // 256).astype(F32)
        feat = jnp.where(off_lanes, jnp.where(is_left, -off, off).astype(BF16),
                         jnp.where(is_left, q_feat[u // 2][0], q_feat[u // 2][1]))
        return jnp.where(lane < D_QK, qm, feat)

    def kt_at(u, d):
        c, _ = chunk_of(d)
        return jnp.concatenate([kt_ref[0, 0, c, u * D_QK:(u + 1) * D_QK, :], k_feat[u // 2]], axis=0)

    def v_at(u, d):
        c, _ = chunk_of(d)
        return v_ref[0, c, :, (u // 2) * LANES:(u // 2 + 1) * LANES]

    def add_at(k, d):
        if d != 0:
            return None
        if k % 2 == 1:
            return bias_scr[0]
        rel = (lax.broadcasted_iota(jnp.int32, (tq, TK), 0)
               - lax.broadcasted_iota(jnp.int32, (tq, TK), 1)).astype(F32)
        bias = -sigma[k // 2] * jnp.abs(rel + (q0 - diag * TK).astype(F32))
        bias_scr[0] = bias
        return bias

    streams = [(functools.partial(q_at, u), functools.partial(kt_at, u), functools.partial(v_at, u))
               for u in range(4)]
    acc, _ = _attend(streams, n_chunks, s_scr, add_at)
    outs = []
    for r in range(2):
        o = _normalised(acc[2 * r]) - lam * _normalised(acc[2 * r + 1])
        msq = jnp.sum(jnp.where(lane < 64, o * o, 0.0), axis=-1, keepdims=True) * (1.0 / D_V)
        outs.append(o * lax.rsqrt(msq + EPS) * sn_ref[...] * (1.0 - lam_init))
    o_ref[0] = _pair_lanes(outs[0], outs[1]).astype(BF16)


def _full_attn_call(kind, q, kt, v, extra=(), lam_init=0.0):
    bsz, s_len = q.shape[0], q.shape[1]
    tq = TQ_D if kind == "d" else TQ
    nq, nc = s_len // tq, s_len // TK
    tok = lambda w: pl.BlockSpec((1, tq, w), lambda b, i: (b, i, 0))
    scratch = [pltpu.VMEM((2, nc, tq, TK), F32)]
    if kind == "b":
        body = functools.partial(_attn_b_kernel, n_chunks=nc)
        in_specs = [tok(512), _per_batch_spec(kt), _per_batch_spec(v)]
    elif kind == "c":
        body = functools.partial(_attn_c_kernel, n_chunks=nc)
        in_specs = [tok(256), _per_batch_spec(kt), _per_batch_spec(v)]
    else:
        return pl.pallas_call(
            functools.partial(_attn_d_kernel, n_chunks=nc, lam_init=lam_init),
            grid=(bsz, 2, nq),
            in_specs=[pl.BlockSpec((1, tq, 512), lambda b, g, i: (b, i, g)),
                      pl.BlockSpec((1, 1, nc, 128, TK), lambda b, g, i: (b, g, 0, 0, 0)),
                      pl.BlockSpec((1, nc, TK, 256), lambda b, g, i: (b, 0, 0, g)),
                      _const_spec((1, D_QK)), _const_spec((1, D_QK)),
                      _const_spec((1, D_QK)), _const_spec((1, D_QK)),
                      _const_spec((1, 128))],
            out_specs=pl.BlockSpec((1, tq, 128), lambda b, g, i: (b, i, g)),
            out_shape=jax.ShapeDtypeStruct((bsz, s_len, 256), BF16),
            scratch_shapes=scratch + [pltpu.VMEM((1, tq, TK), F32)],
            compiler_params=pltpu.CompilerParams(
                dimension_semantics=("arbitrary", "arbitrary", "arbitrary"),
                vmem_limit_bytes=VMEM_LIMIT),
            name="attn_d",
        )(q, kt, v, *extra)
    return pl.pallas_call(
        body,
        grid=(bsz, nq),
        in_specs=in_specs,
        out_specs=tok(256),
        out_shape=jax.ShapeDtypeStruct((bsz, s_len, 256), BF16),
        scratch_shapes=scratch,
        compiler_params=pltpu.CompilerParams(
            dimension_semantics=("arbitrary", "arbitrary"), vmem_limit_bytes=VMEM_LIMIT),
        name="attn_" + kind,
    )(q, kt, v, *extra)


def _attn_a_kernel(q_ref, kt_ref, v_ref, sink_ref, o_ref, s_scr, *, s_len):
    tile = pl.program_id(1)
    q0 = tile * TQA
    n_blocks = s_len // BLOCK
    per_chunk = TKA // BLOCK
    first = tile * (TQA // BLOCK) - 1

    def key_blocks(c):
        return [jnp.clip(first + per_chunk * c + j, 0, n_blocks - 1) for j in range(per_chunk)]

    def kt_at(g, c):
        return jnp.concatenate([kt_ref[0, g, i] for i in key_blocks(c)], axis=1)

    def v_at(g, c):
        return jnp.concatenate(
            [v_ref[0, g, pl.ds(pl.multiple_of(i * BLOCK, BLOCK), BLOCK), :] for i in key_blocks(c)], axis=0)

    neg_dist = []
    for c in range(N_CHUNKS_A):
        col = lax.broadcasted_iota(jnp.int32, (TQA, TKA), 1)
        diff = lax.broadcasted_iota(jnp.int32, (TQA, TKA), 0) - col + (BLOCK - TKA * c)
        kpos = q0 - BLOCK + TKA * c + col
        valid = (jnp.abs(diff) <= WINDOW) & (kpos >= 0) & (kpos < s_len)
        neg_dist.append(jnp.where(valid, -jnp.abs(diff).astype(F32), -jnp.inf))

    streams = []
    for g in range(2):
        for qm in _half_masks(q_ref[0, :, g * 128:(g + 1) * 128]):
            streams.append((qm, functools.partial(kt_at, g), functools.partial(v_at, g)))
    sinks = [sink_ref[:, h:h + 1] * LOG2E for h in range(A_HEADS)]
    acc, m = _attend(streams, N_CHUNKS_A, s_scr,
                     add_at=lambda k, c: (SLOPES[k] * LOG2E) * neg_dist[c], m_floor=sinks)
    outs = []
    for h in range(A_HEADS):
        den = acc[h] + jnp.exp2(sinks[h] - m[h])
        outs.append(acc[h] / den[:, 64:65])
    for g in range(2):
        o_ref[0, :, g * 128:(g + 1) * 128] = _pair_lanes(outs[2 * g], outs[2 * g + 1]).astype(BF16)


def _attn_a_call(q, kt, v, sink):
    bsz, s_len = q.shape[0], q.shape[1]
    tok = pl.BlockSpec((1, TQA, 256), lambda b, i: (b, i, 0))
    return pl.pallas_call(
        functools.partial(_attn_a_kernel, s_len=s_len),
        grid=(bsz, s_len // TQA),
        in_specs=[tok, _per_batch_spec(kt), _per_batch_spec(v), _const_spec((1, A_HEADS))],
        out_specs=tok,
        out_shape=jax.ShapeDtypeStruct((bsz, s_len, 256), BF16),
        scratch_shapes=[pltpu.VMEM((2, N_CHUNKS_A, TQA, TKA), F32)],
        compiler_params=pltpu.CompilerParams(
            dimension_semantics=("arbitrary", "arbitrary"), vmem_limit_bytes=VMEM_LIMIT),
        name="attn_a",
    )(q, kt, v, sink)


def _out_proj_kernel(x_ref, oa_ref, ob_ref, oc_ref, od_ref, wo_ref, x1_ref):
    x1 = x_ref[0]
    for j, o_ref in enumerate((oa_ref, ob_ref, oc_ref, od_ref)):
        x1 = x1 + _dot(o_ref[0], wo_ref[j * 256:(j + 1) * 256, :])
    x1_ref[0] = x1


def _out_proj_call(x, oa, ob, oc, od, lw):
    bsz, s_len, _ = x.shape
    tok = lambda w: pl.BlockSpec((1, TM, w), lambda b, i: (b, i, 0))
    return pl.pallas_call(
        _out_proj_kernel,
        grid=(bsz, s_len // TM),
        in_specs=[tok(D_MODEL), tok(256), tok(256), tok(256), tok(256),
                  _const_spec((D_MODEL, D_MODEL))],
        out_specs=tok(D_MODEL),
        out_shape=jax.ShapeDtypeStruct((bsz, s_len, D_MODEL), F32),
        compiler_params=pltpu.CompilerParams(
            dimension_semantics=("arbitrary", "arbitrary"), vmem_limit_bytes=VMEM_LIMIT),
        name="out_proj",
    )(x, oa, ob, oc, od, lw["w_out"])


def _transpose8(blocks):
    sub = lax.broadcasted_iota(jnp.int32, blocks[0].shape, 0)
    for d in (4, 2, 1):
        keep = (sub & d) == 0
        nxt = list(blocks)
        for i in range(SUBLANES):
            if i & d == 0:
                lo, hi = blocks[i], blocks[i + d]
                nxt[i] = jnp.where(keep, lo, pltpu.roll(hi, d, 0))
                nxt[i + d] = jnp.where(keep, pltpu.roll(lo, SUBLANES - d, 0), hi)
        blocks = nxt
    return blocks


def _ffn_kernel(x1_ref, prev_ref, next_ref, g_ref, wua_ref, wub_ref, vec_ref, wd_ref, gf_ref,
                o_ref, xf_ref, xb_ref, h0_ref, h1_ref, act_ref, *, final_norm):
    first = pl.program_id(1) == 0
    last = pl.program_id(1) == pl.num_programs(1) - 1
    nv = TM // SUBLANES

    sub = lax.broadcasted_iota(jnp.int32, (SUBLANES, D_MODEL), 0)
    for jb in range(nv // SUBLANES):
        blocks = [x1_ref[0, s * nv + jb * SUBLANES:s * nv + (jb + 1) * SUBLANES, :] for s in range(SUBLANES)]
        for jj, rows in enumerate(_transpose8(blocks)):
            j = jb * SUBLANES + jj
            xf_ref[SUBLANES * (j + 1):SUBLANES * (j + 2), :] = rows
            if j == nv - 1:
                xf_ref[0:SUBLANES, :] = jnp.where(
                    sub == 0, prev_ref[0, SUBLANES - 1:SUBLANES, :], pltpu.roll(rows, 1, 0))
            if j == 0:
                xf_ref[TM + SUBLANES:TM + 2 * SUBLANES, :] = jnp.where(
                    sub == SUBLANES - 1, next_ref[0, 0:1, :], pltpu.roll(rows, SUBLANES - 1, 0))
    xb_ref[...] = _rms(xf_ref[...], g_ref[...]).astype(BF16)

    def up(c, h_ref):
        xe = xb_ref[...]
        h_ref[0] = _dot(xe, wua_ref[c])
        h_ref[1] = _dot(xe, wub_ref[c])

    sub_h = lax.broadcasted_iota(jnp.int32, (SUBLANES, FF_CHUNK), 0)

    def conv(h_ref, half, bias, w0, w1, w2, cb):
        top = h_ref[half, 0:SUBLANES, :]
        h_ref[half, 0:SUBLANES, :] = jnp.where(first & (sub_h == 0), -bias, top)
        bot = h_ref[half, TM + SUBLANES:TM + 2 * SUBLANES, :]
        h_ref[half, TM + SUBLANES:TM + 2 * SUBLANES, :] = jnp.where(last & (sub_h == SUBLANES - 1), -bias, bot)
        return (w0 * h_ref[half, 0:TM, :] + w1 * h_ref[half, SUBLANES:SUBLANES + TM, :]
                + w2 * h_ref[half, 2 * SUBLANES:2 * SUBLANES + TM, :] + (bias * (w0 + w1 + w2) + cb))

    def gate(c, h_ref):
        vec = vec_ref[c]
        ca = conv(h_ref, 0, vec[0:1], vec[2:3], vec[3:4], vec[4:5], vec[8:9])
        cb = conv(h_ref, 1, vec[1:2], vec[5:6], vec[6:7], vec[7:8], vec[9:10])
        act_ref[c] = ((ca / (1.0 + jnp.exp(-ca))) * cb).astype(BF16)

    up(0, h0_ref)

    def pair(j, carry):
        up(2 * j + 1, h1_ref)
        gate(2 * j, h0_ref)
        up(2 * j + 2, h0_ref)
        gate(2 * j + 1, h1_ref)
        return carry

    lax.fori_loop(0, (N_FF_CHUNKS - 1) // 2, pair, 0)
    gate(N_FF_CHUNKS - 1, h0_ref)

    y = _dot(act_ref[0], wd_ref[0])
    for c in range(1, N_FF_CHUNKS):
        y = y + _dot(act_ref[c], wd_ref[c])
    xf_ref[0:TM, :] = y
    for jb in range(nv // SUBLANES):
        groups = [xf_ref[(jb * SUBLANES + jj) * SUBLANES:(jb * SUBLANES + jj + 1) * SUBLANES, :]
                  for jj in range(SUBLANES)]
        for s, blk in enumerate(_transpose8(groups)):
            rows = slice(s * nv + jb * SUBLANES, s * nv + (jb + 1) * SUBLANES)
            out = x1_ref[0, rows, :] + blk
            if final_norm:
                out = _rms(out, gf_ref[...])
            o_ref[0, rows, :] = out


def _ffn_call(x1, lw, g_final, final_norm):
    bsz, s_len, _ = x1.shape
    per_tile = TM // SUBLANES
    last = s_len // SUBLANES - 1
    tok = pl.BlockSpec((1, TM, D_MODEL), lambda b, i: (b, i, 0))
    prev = pl.BlockSpec((1, SUBLANES, D_MODEL), lambda b, i: (b, jnp.maximum(i * per_tile - 1, 0), 0))
    nxt = pl.BlockSpec((1, SUBLANES, D_MODEL),
                       lambda b, i: (b, jnp.minimum((i + 1) * per_tile, last), 0))
    rows = TM + 2 * SUBLANES
    return pl.pallas_call(
        functools.partial(_ffn_kernel, final_norm=final_norm),
        grid=(bsz, s_len // TM),
        in_specs=[tok, prev, nxt, _const_spec((1, D_MODEL)),
                  _const_spec((N_FF_CHUNKS, D_MODEL, FF_CHUNK)),
                  _const_spec((N_FF_CHUNKS, D_MODEL, FF_CHUNK)),
                  _const_spec((N_FF_CHUNKS, 16, FF_CHUNK)),
                  _const_spec((N_FF_CHUNKS, FF_CHUNK, D_MODEL)),
                  _const_spec((1, D_MODEL))],
        out_specs=tok,
        out_shape=jax.ShapeDtypeStruct((bsz, s_len, D_MODEL), F32),
        scratch_shapes=[pltpu.VMEM((rows, D_MODEL), F32),
                        pltpu.VMEM((rows, D_MODEL), BF16),
                        pltpu.VMEM((2, rows, FF_CHUNK), F32),
                        pltpu.VMEM((2, rows, FF_CHUNK), F32),
                        pltpu.VMEM((N_FF_CHUNKS, TM, FF_CHUNK), BF16)],
        compiler_params=pltpu.CompilerParams(
            dimension_semantics=("arbitrary", "arbitrary"), vmem_limit_bytes=VMEM_LIMIT),
        name="ffn",
    )(x1, x1, x1, lw["g_ffn"], lw["w_up_a"], lw["w_up_b"], lw["ffn_vec"], lw["w_down"], g_final)


def _pack_layer(p, l):
    pieces = jnp.split(p["w_in"][l], np.cumsum(IN_SIZES)[:-1].tolist(), axis=1)
    aq, ak, av, bcq, bckv, bkr, cq, ck, cv, dq, dk, dv = pieces
    z = lambda n: jnp.zeros((D_MODEL, n), F32)
    w_in = jnp.concatenate([aq, ak, av, bcq, bckv, z(64), bkr, z(32), cq, ck, cv, dq, dk, dv], axis=1)

    wq = p["b_w_q_up"][l].reshape(B_Q_RANK, B_HEADS, B_NOPE + B_ROPE)
    wq = jnp.pad(wq, ((0, 0), (0, 0), (0, 128 - B_NOPE - B_ROPE))).reshape(B_Q_RANK, B_HEADS * 128)
    wkv = p["b_w_kv_up"][l].reshape(B_KV_RANK, B_HEADS, B_NOPE + B_V)
    wk = jnp.pad(wkv[:, :, :B_NOPE], ((0, 0), (0, 0), (0, 128 - B_NOPE))).reshape(B_KV_RANK, B_HEADS * 128)
    wv = wkv[:, :, B_NOPE:].reshape(B_KV_RANK, B_HEADS * B_V)

    w_up = p["w_up"][l]
    chunked = lambda w: w.reshape(D_MODEL, N_FF_CHUNKS, FF_CHUNK).transpose(1, 0, 2)
    halves = lambda v: (v[..., :D_FF], v[..., D_FF:])
    bua, bub = halves(p["b_up"][l])
    cwa, cwb = halves(p["conv_w"][l])
    cba, cbb = halves(p["conv_b"][l])
    rows = [bua, bub, cwa[0], cwa[1], cwa[2], cwb[0], cwb[1], cwb[2], cba, cbb]
    vec = jnp.stack(rows + [jnp.zeros_like(bua)] * (16 - len(rows)), axis=0)
    vec = vec.reshape(16, N_FF_CHUNKS, FF_CHUNK).transpose(1, 0, 2)
    row = lambda v: v.reshape(1, -1)
    return {
        "g_attn": row(p["g_attn"][l]),
        "w_in": w_in.astype(BF16),
        "a_sink": row(p["a_sink"][l]),
        "b_q_norm": row(p["b_q_norm"][l]),
        "w_q_up": wq.astype(BF16),
        "b_kv_norm": row(p["b_kv_norm"][l]),
        "w_kv_up": jnp.concatenate([wk, wv], axis=1).astype(BF16),
        "c_q_norm": row(jnp.tile(p["c_q_norm"][l], 4)),
        "c_k_norm": row(jnp.tile(p["c_k_norm"][l], 2)),
        "d_lambda": tuple(row(p[n][l]) for n in ("d_lambda_q1", "d_lambda_k1", "d_lambda_q2", "d_lambda_k2")),
        "d_sub_norm": row(jnp.tile(p["d_sub_norm"][l], 2)),
        "w_out": p["w_out"][l].astype(BF16),
        "g_ffn": row(p["g_ffn"][l]),
        "w_up_a": chunked(w_up[:, :D_FF]).astype(BF16),
        "w_up_b": chunked(w_up[:, D_FF:]).astype(BF16),
        "ffn_vec": vec,
        "w_down": p["w_down"][l].reshape(N_FF_CHUNKS, FF_CHUNK, D_MODEL).astype(BF16),
    }


def _rope_tables(s_len):
    half = 16
    inv = ROPE_THETA ** (-jnp.arange(half, dtype=F32) * 2.0 / (2 * half))
    pos = jnp.arange(s_len, dtype=F32)
    rows = s_len // GRID_W
    row_pos = jnp.broadcast_to(jnp.arange(rows, dtype=F32)[:, None], (rows, GRID_W)).reshape(s_len)
    col_pos = jnp.broadcast_to(jnp.arange(GRID_W, dtype=F32)[None, :], (rows, GRID_W)).reshape(s_len)

    def cs(p):
        ang = p[:, None] * inv[None, :]
        return jnp.cos(ang), jnp.sin(ang)

    one = lambda n: jnp.ones((s_len, n), F32)
    zero = lambda n: jnp.zeros((s_len, n), F32)
    c, s = cs(pos)
    rope_b = jnp.stack([
        jnp.concatenate([one(64), c, c, one(32)], axis=1),
        jnp.concatenate([zero(64), -s, zero(16), zero(32)], axis=1),
        jnp.concatenate([zero(64), zero(16), s, zero(32)], axis=1)])
    cr, sr = cs(row_pos)
    cc, sc = cs(col_pos)
    z = zero(16)
    rope_c = jnp.stack([
        jnp.tile(jnp.concatenate([cr, cr, cc, cc], axis=1), (1, 2)),
        jnp.tile(jnp.concatenate([-sr, z, -sc, z], axis=1), (1, 2)),
        jnp.tile(jnp.concatenate([z, sr, z, sc], axis=1), (1, 2))])
    return rope_b, rope_c


def _trunk(x, layers, g_final):
    bsz, s_len, _ = x.shape
    nc = s_len // TK
    rope_b, rope_c = _rope_tables(s_len)
    depth = len(layers)
    for l, lw in enumerate(layers):
        qa, ka, va, qb, ktb, vb, qc, ktc, vc, qd, ktd, vd = _proj_call(x, lw, rope_b, rope_c)
        oa = _attn_a_call(qa, ka, va, lw["a_sink"])
        ob = _full_attn_call("b", qb, ktb, vb.reshape(bsz, nc, TK, 512))
        oc = _full_attn_call("c", qc, ktc, vc.reshape(bsz, 2, nc, TK, 128))
        lam_init = 0.8 - 0.6 * math.exp(-0.3 * l)
        od = _full_attn_call("d", qd, ktd, vd.reshape(bsz, nc, TK, 512),
                             extra=(*lw["d_lambda"], lw["d_sub_norm"]), lam_init=lam_init)
        x1 = _out_proj_call(x, oa, ob, oc, od, lw)
        x = _ffn_call(x1, lw, g_final, final_norm=(l == depth - 1))
    return x


def kernel(x_prompt, x_sample, g_attn, w_in, a_sink, b_q_norm, b_w_q_up, b_kv_norm, b_w_kv_up,
           c_q_norm, c_k_norm, d_lambda_q1, d_lambda_k1, d_lambda_q2, d_lambda_k2, d_sub_norm, w_out,
           g_ffn, w_up, b_up, conv_w, conv_b, w_down, g_final):
    p = dict(g_attn=g_attn, w_in=w_in, a_sink=a_sink, b_q_norm=b_q_norm, b_w_q_up=b_w_q_up,
             b_kv_norm=b_kv_norm, b_w_kv_up=b_w_kv_up, c_q_norm=c_q_norm, c_k_norm=c_k_norm,
             d_lambda_q1=d_lambda_q1, d_lambda_k1=d_lambda_k1, d_lambda_q2=d_lambda_q2,
             d_lambda_k2=d_lambda_k2, d_sub_norm=d_sub_norm, w_out=w_out, g_ffn=g_ffn, w_up=w_up,
             b_up=b_up, conv_w=conv_w, conv_b=conv_b, w_down=w_down)
    layers = [_pack_layer(p, l) for l in range(g_attn.shape[0])]
    gf = g_final.reshape(1, -1)
    return _trunk(x_prompt, layers, gf), _trunk(x_sample, layers, gf)
```

```python
import functools
import math

import jax
import jax.numpy as jnp
import numpy as np
from jax import lax
from jax.experimental import pallas as pl
from jax.experimental.pallas import tpu as pltpu

F32 = jnp.float32
BF16 = jnp.bfloat16

D_MODEL = 1024
GRID_W = 64
BLOCK = 128
HEAD_DIM = 64
EPS = 1e-6
ROPE_THETA = 10000.0

A_HEADS = 4
WINDOW = 128
B_HEADS = 4
B_Q_RANK = 256
B_KV_RANK = 128
B_NOPE = 64
B_ROPE = 32
B_V = 64
D_QK = 32
D_V = 64
N_ALIBI = 8
D_FF = 2816
IN_SIZES = (256, 128, 128, 256, 128, 32, 256, 128, 128, 256, 256, 256)

SLOPES = tuple(2.0 ** (-8.0 * (i + 1.0) / N_ALIBI) for i in range(N_ALIBI))
LOG2E = math.log2(math.e)


def _split3(x):
    rest, out = np.float32(x), []
    for _ in range(3):
        part = np.float32(np.asarray(rest).astype(jnp.bfloat16))
        out.append(float(part))
        rest = np.float32(rest - part)
    return tuple(out)


D_SLOPE_PARTS = tuple(_split3(SLOPES[A_HEADS + h] * LOG2E) for h in range(4))

LANES = 128
TM = 512
TK = 512
TQ = 512
TQ_D = 256
TQA = 512
TKA = 512
N_CHUNKS_A = -(-(TQA + 2 * WINDOW) // TKA)
SUBLANES = 8
FF_CHUNK = 256
N_FF_CHUNKS = D_FF // FF_CHUNK
VMEM_LIMIT = 56 * 1024 * 1024

_SEC_A, _SEC_B, _SEC_C, _SEC_D = (0, 512), (512, 1024), (1024, 1536), (1536, 2304)
_OFF_AQ, _OFF_AK, _OFF_AV = 0, 256, 384
_OFF_BCQ, _OFF_BCKV, _OFF_BKR = 0, 256, 384
_OFF_CQ, _OFF_CK, _OFF_CV = 0, 256, 384
_OFF_DQ, _OFF_DK, _OFF_DV = 0, 256, 512
IN_PACKED = 2304


def _dot(a, b):
    return jnp.dot(a, b, preferred_element_type=F32)


def _rms(x, g):
    return x * lax.rsqrt(jnp.mean(x * x, axis=-1, keepdims=True) + EPS) * g


def _rope(t, tab_ref):
    return (t * tab_ref[0] + pltpu.roll(t, LANES - 16, 1) * tab_ref[1]
            + pltpu.roll(t, 16, 1) * tab_ref[2])


def _dup_halves(t):
    lane = lax.broadcasted_iota(jnp.int32, t.shape, 1)
    sw = pltpu.roll(t, 64, 1)
    lo = lane < 64
    return jnp.where(lo, t, sw), jnp.where(lo, sw, t)


def _with_ones(t):
    lane = lax.broadcasted_iota(jnp.int32, t.shape, 1)
    lo = lane < 64
    return jnp.where(lo, t, 1.0), jnp.where(lo, pltpu.roll(t, 64, 1), 1.0)


def _head_mean_sq(t):
    w = t.shape[1]
    r = lax.broadcasted_iota(jnp.int32, (w, w), 0) // HEAD_DIM
    c = lax.broadcasted_iota(jnp.int32, (w, w), 1) // HEAD_DIM
    pool = jnp.where(r == c, 1.0 / HEAD_DIM, 0.0).astype(BF16)
    t2 = t * t
    hi = t2.astype(BF16)
    lo = (t2 - hi.astype(F32)).astype(BF16)
    return _dot(hi, pool) + _dot(lo, pool)


def _proj_kernel(x_ref, g_ref, win_ref, bqn_ref, wq_ref, bkvn_ref, wkv_ref, cqn_ref, ckn_ref,
                 ropeb_ref, ropec_ref,
                 qa_ref, ka_ref, va_ref, qb_ref, ktb_ref, vb_ref, qc_ref, ktc_ref, vc_ref,
                 qd_ref, ktd_ref, vd_ref):
    x = x_ref[0]
    xn = _rms(x, g_ref[...]).astype(BF16)
    h_all = _dot(xn, win_ref[...])
    project = lambda sec: h_all[:, sec[0]:sec[1]]

    h = project(_SEC_A)
    qa_ref[0] = (h[:, _OFF_AQ:_OFF_AQ + 256] * (HEAD_DIM ** -0.5 * LOG2E)).astype(BF16)
    for grp, kd in enumerate(_dup_halves(h[:, _OFF_AK:_OFF_AK + 128])):
        for j in range(TM // BLOCK):
            ka_ref[0, grp, j] = kd[j * BLOCK:(j + 1) * BLOCK].T.astype(BF16)
    v0, v1 = _with_ones(h[:, _OFF_AV:_OFF_AV + 128])
    va_ref[0, 0] = v0.astype(BF16)
    va_ref[0, 1] = v1.astype(BF16)

    h = project(_SEC_B)
    cq = _rms(h[:, _OFF_BCQ:_OFF_BCQ + B_Q_RANK], bqn_ref[...]).astype(BF16)
    qb = _dot(cq, wq_ref[...])
    scale_b = (B_NOPE + B_ROPE) ** -0.5 * LOG2E
    for hd in range(B_HEADS):
        t = _rope(qb[:, hd * 128:(hd + 1) * 128], ropeb_ref)
        qb_ref[0, :, hd * 128:(hd + 1) * 128] = (t * scale_b).astype(BF16)
    ckv = _rms(h[:, _OFF_BCKV:_OFF_BCKV + B_KV_RANK], bkvn_ref[...]).astype(BF16)
    kv = _dot(ckv, wkv_ref[...])
    kr = _rope(h[:, _OFF_BKR:_OFF_BKR + 128], ropeb_ref)
    for hd in range(B_HEADS):
        ktb_ref[0, hd, 0] = (kv[:, hd * 128:(hd + 1) * 128] + kr).T.astype(BF16)
    for grp in range(2):
        v0, v1 = _with_ones(kv[:, 512 + grp * 128:512 + (grp + 1) * 128])
        vb_ref[0, :, (2 * grp) * 128:(2 * grp + 1) * 128] = v0.astype(BF16)
        vb_ref[0, :, (2 * grp + 1) * 128:(2 * grp + 2) * 128] = v1.astype(BF16)

    h = project(_SEC_C)
    cqh = h[:, _OFF_CQ:_OFF_CQ + 256]
    cqh = cqh * lax.rsqrt(_head_mean_sq(cqh) + EPS) * cqn_ref[...]
    for grp in range(2):
        t = _rope(cqh[:, grp * 128:(grp + 1) * 128], ropec_ref)
        qc_ref[0, :, grp * 128:(grp + 1) * 128] = (t * (HEAD_DIM ** -0.5 * LOG2E)).astype(BF16)
    ckh = h[:, _OFF_CK:_OFF_CK + 128]
    ckh = ckh * lax.rsqrt(_head_mean_sq(ckh) + EPS) * ckn_ref[...]
    ckh = _rope(ckh, ropec_ref)
    k0, k1 = _dup_halves(ckh)
    ktc_ref[0, 0, 0] = k0.T.astype(BF16)
    ktc_ref[0, 1, 0] = k1.T.astype(BF16)
    v0, v1 = _with_ones(h[:, _OFF_CV:_OFF_CV + 128])
    vc_ref[0, 0] = v0.astype(BF16)
    vc_ref[0, 1] = v1.astype(BF16)

    h = project(_SEC_D)
    lane = lax.broadcasted_iota(jnp.int32, (TM, LANES), 1)
    for grp in range(2):
        qg = h[:, _OFF_DQ + grp * 128:_OFF_DQ + (grp + 1) * 128] * (D_QK ** -0.5 * LOG2E)
        for u in range(4):
            t = qg if u == 0 else pltpu.roll(qg, LANES - D_QK * u, 1)
            qd_ref[0, :, (4 * grp + u) * LANES:(4 * grp + u + 1) * LANES] = (
                jnp.where(lane < D_QK, t, 0.0).astype(BF16))
    for grp in range(2):
        ktd_ref[0, grp, 0] = h[:, _OFF_DK + grp * 128:_OFF_DK + (grp + 1) * 128].T.astype(BF16)
    for grp in range(2):
        v0, v1 = _with_ones(h[:, _OFF_DV + grp * 128:_OFF_DV + (grp + 1) * 128])
        vd_ref[0, :, (2 * grp) * 128:(2 * grp + 1) * 128] = v0.astype(BF16)
        vd_ref[0, :, (2 * grp + 1) * 128:(2 * grp + 2) * 128] = v1.astype(BF16)


def _const_spec(shape):
    nd = len(shape)
    return pl.BlockSpec(shape, lambda *_: (0,) * nd)


def _per_batch_spec(a):
    return pl.BlockSpec((1,) + a.shape[1:], lambda b, i: (b,) + (0,) * (a.ndim - 1))


def _proj_call(x, lw, rope_b, rope_c):
    bsz, s_len, _ = x.shape
    nt = s_len // TM
    tok = lambda w: pl.BlockSpec((1, TM, w), lambda b, i: (b, i, 0))
    dup = pl.BlockSpec((1, 2, TM, 128), lambda b, i: (b, 0, i, 0))
    kt = lambda n: pl.BlockSpec((1, n, 1, 128, TK), lambda b, i: (b, 0, i, 0, 0))
    sds = jax.ShapeDtypeStruct
    out_shape = (
        sds((bsz, s_len, 256), BF16), sds((bsz, 2, s_len // BLOCK, 128, BLOCK), BF16),
        sds((bsz, 2, s_len, 128), BF16),
        sds((bsz, s_len, 512), BF16), sds((bsz, 4, nt, 128, TK), BF16), sds((bsz, s_len, 512), BF16),
        sds((bsz, s_len, 256), BF16), sds((bsz, 2, nt, 128, TK), BF16), sds((bsz, 2, s_len, 128), BF16),
        sds((bsz, s_len, 1024), BF16), sds((bsz, 2, nt, 128, TK), BF16), sds((bsz, s_len, 512), BF16),
    )
    kta = pl.BlockSpec((1, 2, TM // BLOCK, 128, BLOCK), lambda b, i: (b, 0, i, 0, 0))
    out_specs = (tok(256), kta, dup, tok(512), kt(4), tok(512),
                 tok(256), kt(2), dup, tok(1024), kt(2), tok(512))
    rope_spec = pl.BlockSpec((3, TM, 128), lambda b, i: (0, i, 0))
    in_specs = [
        tok(D_MODEL), _const_spec((1, D_MODEL)), _const_spec((D_MODEL, IN_PACKED)),
        _const_spec((1, B_Q_RANK)), _const_spec((B_Q_RANK, 512)),
        _const_spec((1, B_KV_RANK)), _const_spec((B_KV_RANK, 768)),
        _const_spec((1, 256)), _const_spec((1, 128)),
        rope_spec, rope_spec,
    ]
    return pl.pallas_call(
        _proj_kernel,
        grid=(bsz, nt),
        in_specs=in_specs,
        out_specs=out_specs,
        out_shape=out_shape,
        compiler_params=pltpu.CompilerParams(
            dimension_semantics=("arbitrary", "arbitrary"), vmem_limit_bytes=VMEM_LIMIT),
        name="in_proj",
    )(x, lw["g_attn"], lw["w_in"], lw["b_q_norm"], lw["w_q_up"], lw["b_kv_norm"], lw["w_kv_up"],
      lw["c_q_norm"], lw["c_k_norm"], rope_b, rope_c)


def _fold(op, s):
    out = s[:, 0:LANES]
    for j in range(1, s.shape[1] // LANES):
        out = op(out, s[:, j * LANES:(j + 1) * LANES])
    return out


def _attend(streams, n_chunks, s_scr, add_at=None, m_floor=None):
    n = len(streams)
    rows = s_scr.shape[2]
    m_run = [jnp.full((rows, LANES), -jnp.inf, F32)] * n
    acc = [jnp.zeros((rows, LANES), F32)] * n
    m = [None] * n
    for k in range(n + 1):
        for c in range(n_chunks):
            if k < n:
                qm, kt_at, _ = streams[k]
                s = _dot(qm(c) if callable(qm) else qm, kt_at(c))
                bias = None if add_at is None else add_at(k, c)
                if bias is not None:
                    s = s + bias
                s_scr[k % 2, c] = s
                m_run[k] = jnp.maximum(m_run[k], _fold(jnp.maximum, s))
            if k >= 1:
                j = k - 1
                p = jnp.exp2((s_scr[j % 2, c] - m[j]).astype(BF16))
                acc[j] = acc[j] + _dot(p, streams[j][2](c))
        if k < n:
            m[k] = jnp.max(m_run[k], axis=-1, keepdims=True)
            if m_floor is not None:
                m[k] = jnp.maximum(m[k], m_floor[k])
    return acc, m


def _normalised(acc):
    return acc / acc[:, 64:65]


def _pair_lanes(o0, o1):
    lane = lax.broadcasted_iota(jnp.int32, o0.shape, 1)
    return jnp.where(lane < 64, o0, pltpu.roll(o1, 64, 1))


def _half_masks(q):
    lane = lax.broadcasted_iota(jnp.int32, q.shape, 1)
    zero = jnp.zeros_like(q)
    return jnp.where(lane < 64, q, zero), jnp.where(lane < 64, zero, q)


def _attn_b_kernel(q_ref, kt_ref, v_ref, o_ref, s_scr, *, n_chunks):
    streams = [(q_ref[0, :, h * 128:(h + 1) * 128], lambda c, h=h: kt_ref[0, h, c],
                lambda c, h=h: v_ref[0, c, :, h * 128:(h + 1) * 128]) for h in range(B_HEADS)]
    acc, _ = _attend(streams, n_chunks, s_scr)
    for g in range(2):
        o_ref[0, :, g * 128:(g + 1) * 128] = _pair_lanes(
            _normalised(acc[2 * g]), _normalised(acc[2 * g + 1])).astype(BF16)


def _attn_c_kernel(q_ref, kt_ref, v_ref, o_ref, s_scr, *, n_chunks):
    streams = []
    for g in range(2):
        for qm in _half_masks(q_ref[0, :, g * 128:(g + 1) * 128]):
            streams.append((qm, lambda c, g=g: kt_ref[0, g, c], lambda c, g=g: v_ref[0, g, c]))
    acc, _ = _attend(streams, n_chunks, s_scr)
    for g in range(2):
        o_ref[0, :, g * 128:(g + 1) * 128] = _pair_lanes(
            _normalised(acc[2 * g]), _normalised(acc[2 * g + 1])).astype(BF16)


def _attn_d_kernel(q_ref, kt_ref, v_ref, lq1_ref, lk1_ref, lq2_ref, lk2_ref, sn_ref, o_ref, s_scr,
                   bias_scr, *, n_chunks, lam_init):
    grp = pl.program_id(1)
    tq = q_ref.shape[1]
    q0 = pl.program_id(2) * tq
    diag = q0 // TK
    lam = (jnp.exp(jnp.sum(lq1_ref[...] * lk1_ref[...], axis=-1, keepdims=True))
           - jnp.exp(jnp.sum(lq2_ref[...] * lk2_ref[...], axis=-1, keepdims=True)) + lam_init)
    lane = lax.broadcasted_iota(jnp.int32, (tq, LANES), 1)
    row = lax.broadcasted_iota(jnp.int32, (tq, LANES), 0).astype(F32)
    parts = [[jnp.where(grp == 0, D_SLOPE_PARTS[r][p], D_SLOPE_PARTS[2 + r][p]).astype(F32)
              for p in range(3)] for r in range(2)]

    f = lane - D_QK
    third = lambda idx, p: jnp.where(idx % 3 == 0, p[0], jnp.where(idx % 3 == 1, p[1], p[2]))
    q_feat, k_feat, sigma = [], [], []
    kr = lax.broadcasted_iota(jnp.int32, (LANES - D_QK, TK), 0)
    kc = lax.broadcasted_iota(jnp.int32, (LANES - D_QK, TK), 1)
    for r in range(2):
        left = jnp.where((f >= 0) & (f < 6), third(f, parts[r]), jnp.where((f >= 6) & (f < 9), -row, 0.0))
        q_feat.append((left.astype(BF16), (-left).astype(BF16)))
        sig_rows = third(kr, parts[r])
        k_feat.append(jnp.where(kr < 3, (kc % 256).astype(F32),
                      jnp.where(kr < 6, (kc // 256 * 256).astype(F32),
                      jnp.where(kr < 9, sig_rows,
                      jnp.where(kr < 12, 256.0 * sig_rows, 0.0)))).astype(BF16))
        sigma.append(parts[r][0] + parts[r][1] + parts[r][2])
    off_lanes = (f >= 9) & (f < 12)

    def chunk_of(d):
        wrapped = diag + d >= n_chunks
        return jnp.where(wrapped, diag + d - n_chunks, diag + d), wrapped

    def q_at(u, d):
        qm = q_ref[0, :, u * LANES:(u + 1) * LANES]
        if d == 0:
            return qm
        c, is_left = chunk_of(d)
        off = ((q0 - c * TK) // 256).astype(F32)
        feat = jnp.where(off_lanes, jnp.where(is_left, -off, off).astype(BF16),
                         jnp.where(is_left, q_feat[u // 2][0], q_feat[u // 2][1]))
        return jnp.where(lane < D_QK, qm, feat)

    def kt_at(u, d):
        c, _ = chunk_of(d)
        return jnp.concatenate([kt_ref[0, 0, c, u * D_QK:(u + 1) * D_QK, :], k_feat[u // 2]], axis=0)

    def v_at(u, d):
        c, _ = chunk_of(d)
        return v_ref[0, c, :, (u // 2) * LANES:(u // 2 + 1) * LANES]

    def add_at(k, d):
        if d != 0:
            return None
        if k % 2 == 1:
            return bias_scr[0]
        rel = (lax.broadcasted_iota(jnp.int32, (tq, TK), 0)
               - lax.broadcasted_iota(jnp.int32, (tq, TK), 1)).astype(F32)
        bias = -sigma[k // 2] * jnp.abs(rel + (q0 - diag * TK).astype(F32))
        bias_scr[0] = bias
        return bias

    streams = [(functools.partial(q_at, u), functools.partial(kt_at, u), functools.partial(v_at, u))
               for u in range(4)]
    acc, _ = _attend(streams, n_chunks, s_scr, add_at)
    outs = []
    for r in range(2):
        o = _normalised(acc[2 * r]) - lam * _normalised(acc[2 * r + 1])
        msq = jnp.sum(jnp.where(lane < 64, o * o, 0.0), axis=-1, keepdims=True) * (1.0 / D_V)
        outs.append(o * lax.rsqrt(msq + EPS) * sn_ref[...] * (1.0 - lam_init))
    o_ref[0] = _pair_lanes(outs[0], outs[1]).astype(BF16)


def _full_attn_call(kind, q, kt, v, extra=(), lam_init=0.0):
    bsz, s_len = q.shape[0], q.shape[1]
    tq = TQ_D if kind == "d" else TQ
    nq, nc = s_len // tq, s_len // TK
    tok = lambda w: pl.BlockSpec((1, tq, w), lambda b, i: (b, i, 0))
    scratch = [pltpu.VMEM((2, nc, tq, TK), F32)]
    if kind == "b":
        body = functools.partial(_attn_b_kernel, n_chunks=nc)
        in_specs = [tok(512), _per_batch_spec(kt), _per_batch_spec(v)]
    elif kind == "c":
        body = functools.partial(_attn_c_kernel, n_chunks=nc)
        in_specs = [tok(256), _per_batch_spec(kt), _per_batch_spec(v)]
    else:
        return pl.pallas_call(
            functools.partial(_attn_d_kernel, n_chunks=nc, lam_init=lam_init),
            grid=(bsz, 2, nq),
            in_specs=[pl.BlockSpec((1, tq, 512), lambda b, g, i: (b, i, g)),
                      pl.BlockSpec((1, 1, nc, 128, TK), lambda b, g, i: (b, g, 0, 0, 0)),
                      pl.BlockSpec((1, nc, TK, 256), lambda b, g, i: (b, 0, 0, g)),
                      _const_spec((1, D_QK)), _const_spec((1, D_QK)),
                      _const_spec((1, D_QK)), _const_spec((1, D_QK)),
                      _const_spec((1, 128))],
            out_specs=pl.BlockSpec((1, tq, 128), lambda b, g, i: (b, i, g)),
            out_shape=jax.ShapeDtypeStruct((bsz, s_len, 256), BF16),
            scratch_shapes=scratch + [pltpu.VMEM((1, tq, TK), F32)],
            compiler_params=pltpu.CompilerParams(
                dimension_semantics=("arbitrary", "arbitrary", "arbitrary"),
                vmem_limit_bytes=VMEM_LIMIT),
            name="attn_d",
        )(q, kt, v, *extra)
    return pl.pallas_call(
        body,
        grid=(bsz, nq),
        in_specs=in_specs,
        out_specs=tok(256),
        out_shape=jax.ShapeDtypeStruct((bsz, s_len, 256), BF16),
        scratch_shapes=scratch,
        compiler_params=pltpu.CompilerParams(
            dimension_semantics=("arbitrary", "arbitrary"), vmem_limit_bytes=VMEM_LIMIT),
        name="attn_" + kind,
    )(q, kt, v, *extra)


def _attn_a_kernel(q_ref, kt_ref, v_ref, sink_ref, o_ref, s_scr, *, s_len):
    tile = pl.program_id(1)
    q0 = tile * TQA
    n_blocks = s_len // BLOCK
    per_chunk = TKA // BLOCK
    first = tile * (TQA // BLOCK) - 1

    def key_blocks(c):
        return [jnp.clip(first + per_chunk * c + j, 0, n_blocks - 1) for j in range(per_chunk)]

    def kt_at(g, c):
        return jnp.concatenate([kt_ref[0, g, i] for i in key_blocks(c)], axis=1)

    def v_at(g, c):
        return jnp.concatenate(
            [v_ref[0, g, pl.ds(pl.multiple_of(i * BLOCK, BLOCK), BLOCK), :] for i in key_blocks(c)], axis=0)

    neg_dist = []
    for c in range(N_CHUNKS_A):
        col = lax.broadcasted_iota(jnp.int32, (TQA, TKA), 1)
        diff = lax.broadcasted_iota(jnp.int32, (TQA, TKA), 0) - col + (BLOCK - TKA * c)
        kpos = q0 - BLOCK + TKA * c + col
        valid = (jnp.abs(diff) <= WINDOW) & (kpos >= 0) & (kpos < s_len)
        neg_dist.append(jnp.where(valid, -jnp.abs(diff).astype(F32), -jnp.inf))

    streams = []
    for g in range(2):
        for qm in _half_masks(q_ref[0, :, g * 128:(g + 1) * 128]):
            streams.append((qm, functools.partial(kt_at, g), functools.partial(v_at, g)))
    sinks = [sink_ref[:, h:h + 1] * LOG2E for h in range(A_HEADS)]
    acc, m = _attend(streams, N_CHUNKS_A, s_scr,
                     add_at=lambda k, c: (SLOPES[k] * LOG2E) * neg_dist[c], m_floor=sinks)
    outs = []
    for h in range(A_HEADS):
        den = acc[h] + jnp.exp2(sinks[h] - m[h])
        outs.append(acc[h] / den[:, 64:65])
    for g in range(2):
        o_ref[0, :, g * 128:(g + 1) * 128] = _pair_lanes(outs[2 * g], outs[2 * g + 1]).astype(BF16)


def _attn_a_call(q, kt, v, sink):
    bsz, s_len = q.shape[0], q.shape[1]
    tok = pl.BlockSpec((1, TQA, 256), lambda b, i: (b, i, 0))
    return pl.pallas_call(
        functools.partial(_attn_a_kernel, s_len=s_len),
        grid=(bsz, s_len // TQA),
        in_specs=[tok, _per_batch_spec(kt), _per_batch_spec(v), _const_spec((1, A_HEADS))],
        out_specs=tok,
        out_shape=jax.ShapeDtypeStruct((bsz, s_len, 256), BF16),
        scratch_shapes=[pltpu.VMEM((2, N_CHUNKS_A, TQA, TKA), F32)],
        compiler_params=pltpu.CompilerParams(
            dimension_semantics=("arbitrary", "arbitrary"), vmem_limit_bytes=VMEM_LIMIT),
        name="attn_a",
    )(q, kt, v, sink)


def _out_proj_kernel(x_ref, oa_ref, ob_ref, oc_ref, od_ref, wo_ref, x1_ref):
    x1 = x_ref[0]
    for j, o_ref in enumerate((oa_ref, ob_ref, oc_ref, od_ref)):
        x1 = x1 + _dot(o_ref[0], wo_ref[j * 256:(j + 1) * 256, :])
    x1_ref[0] = x1


def _out_proj_call(x, oa, ob, oc, od, lw):
    bsz, s_len, _ = x.shape
    tok = lambda w: pl.BlockSpec((1, TM, w), lambda b, i: (b, i, 0))
    return pl.pallas_call(
        _out_proj_kernel,
        grid=(bsz, s_len // TM),
        in_specs=[tok(D_MODEL), tok(256), tok(256), tok(256), tok(256),
                  _const_spec((D_MODEL, D_MODEL))],
        out_specs=tok(D_MODEL),
        out_shape=jax.ShapeDtypeStruct((bsz, s_len, D_MODEL), F32),
        compiler_params=pltpu.CompilerParams(
            dimension_semantics=("arbitrary", "arbitrary"), vmem_limit_bytes=VMEM_LIMIT),
        name="out_proj",
    )(x, oa, ob, oc, od, lw["w_out"])


def _transpose8(blocks):
    sub = lax.broadcasted_iota(jnp.int32, blocks[0].shape, 0)
    for d in (4, 2, 1):
        keep = (sub & d) == 0
        nxt = list(blocks)
        for i in range(SUBLANES):
            if i & d == 0:
                lo, hi = blocks[i], blocks[i + d]
                nxt[i] = jnp.where(keep, lo, pltpu.roll(hi, d, 0))
                nxt[i + d] = jnp.where(keep, pltpu.roll(lo, SUBLANES - d, 0), hi)
        blocks = nxt
    return blocks


def _ffn_kernel(x1_ref, prev_ref, next_ref, g_ref, wua_ref, wub_ref, vec_ref, wd_ref, gf_ref,
                o_ref, xf_ref, xb_ref, h0_ref, h1_ref, act_ref, *, final_norm):
    first = pl.program_id(1) == 0
    last = pl.program_id(1) == pl.num_programs(1) - 1
    nv = TM // SUBLANES

    sub = lax.broadcasted_iota(jnp.int32, (SUBLANES, D_MODEL), 0)
    for jb in range(nv // SUBLANES):
        blocks = [x1_ref[0, s * nv + jb * SUBLANES:s * nv + (jb + 1) * SUBLANES, :] for s in range(SUBLANES)]
        for jj, rows in enumerate(_transpose8(blocks)):
            j = jb * SUBLANES + jj
            xf_ref[SUBLANES * (j + 1):SUBLANES * (j + 2), :] = rows
            if j == nv - 1:
                xf_ref[0:SUBLANES, :] = jnp.where(
                    sub == 0, prev_ref[0, SUBLANES - 1:SUBLANES, :], pltpu.roll(rows, 1, 0))
            if j == 0:
                xf_ref[TM + SUBLANES:TM + 2 * SUBLANES, :] = jnp.where(
                    sub == SUBLANES - 1, next_ref[0, 0:1, :], pltpu.roll(rows, SUBLANES - 1, 0))
    xb_ref[...] = _rms(xf_ref[...], g_ref[...]).astype(BF16)

    def up(c, h_ref):
        xe = xb_ref[...]
        h_ref[0] = _dot(xe, wua_ref[c])
        h_ref[1] = _dot(xe, wub_ref[c])

    sub_h = lax.broadcasted_iota(jnp.int32, (SUBLANES, FF_CHUNK), 0)

    def conv(h_ref, half, bias, w0, w1, w2, cb):
        top = h_ref[half, 0:SUBLANES, :]
        h_ref[half, 0:SUBLANES, :] = jnp.where(first & (sub_h == 0), -bias, top)
        bot = h_ref[half, TM + SUBLANES:TM + 2 * SUBLANES, :]
        h_ref[half, TM + SUBLANES:TM + 2 * SUBLANES, :] = jnp.where(last & (sub_h == SUBLANES - 1), -bias, bot)
        return (w0 * h_ref[half, 0:TM, :] + w1 * h_ref[half, SUBLANES:SUBLANES + TM, :]
                + w2 * h_ref[half, 2 * SUBLANES:2 * SUBLANES + TM, :] + (bias * (w0 + w1 + w2) + cb))

    def gate(c, h_ref):
        vec = vec_ref[c]
        ca = conv(h_ref, 0, vec[0:1], vec[2:3], vec[3:4], vec[4:5], vec[8:9])
        cb = conv(h_ref, 1, vec[1:2], vec[5:6], vec[6:7], vec[7:8], vec[9:10])
        act_ref[c] = ((ca / (1.0 + jnp.exp(-ca))) * cb).astype(BF16)

    up(0, h0_ref)

    def pair(j, carry):
        up(2 * j + 1, h1_ref)
        gate(2 * j, h0_ref)
        up(2 * j + 2, h0_ref)
        gate(2 * j + 1, h1_ref)
        return carry

    lax.fori_loop(0, (N_FF_CHUNKS - 1) // 2, pair, 0, unroll=True)
    gate(N_FF_CHUNKS - 1, h0_ref)

    for n in range(D_MODEL // FF_CHUNK):
        cols = slice(n * FF_CHUNK, (n + 1) * FF_CHUNK)
        y = _dot(act_ref[0], wd_ref[0, :, cols])
        for c in range(1, N_FF_CHUNKS):
            y = y + _dot(act_ref[c], wd_ref[c, :, cols])
        xf_ref[0:TM, cols] = y
        for jb in range(nv // SUBLANES):
            groups = [xf_ref[(jb * SUBLANES + jj) * SUBLANES:(jb * SUBLANES + jj + 1) * SUBLANES, cols]
                      for jj in range(SUBLANES)]
            for s, blk in enumerate(_transpose8(groups)):
                rows = slice(s * nv + jb * SUBLANES, s * nv + (jb + 1) * SUBLANES)
                o_ref[0, rows, cols] = x1_ref[0, rows, cols] + blk
    if final_norm:
        o_ref[0] = _rms(o_ref[0], gf_ref[...])


def _ffn_call(x1, lw, g_final, final_norm):
    bsz, s_len, _ = x1.shape
    per_tile = TM // SUBLANES
    last = s_len // SUBLANES - 1
    tok = pl.BlockSpec((1, TM, D_MODEL), lambda b, i: (b, i, 0))
    prev = pl.BlockSpec((1, SUBLANES, D_MODEL), lambda b, i: (b, jnp.maximum(i * per_tile - 1, 0), 0))
    nxt = pl.BlockSpec((1, SUBLANES, D_MODEL),
                       lambda b, i: (b, jnp.minimum((i + 1) * per_tile, last), 0))
    rows = TM + 2 * SUBLANES
    return pl.pallas_call(
        functools.partial(_ffn_kernel, final_norm=final_norm),
        grid=(bsz, s_len // TM),
        in_specs=[tok, prev, nxt, _const_spec((1, D_MODEL)),
                  _const_spec((N_FF_CHUNKS, D_MODEL, FF_CHUNK)),
                  _const_spec((N_FF_CHUNKS, D_MODEL, FF_CHUNK)),
                  _const_spec((N_FF_CHUNKS, 16, FF_CHUNK)),
                  _const_spec((N_FF_CHUNKS, FF_CHUNK, D_MODEL)),
                  _const_spec((1, D_MODEL))],
        out_specs=tok,
        out_shape=jax.ShapeDtypeStruct((bsz, s_len, D_MODEL), F32),
        scratch_shapes=[pltpu.VMEM((rows, D_MODEL), F32),
                        pltpu.VMEM((rows, D_MODEL), BF16),
                        pltpu.VMEM((2, rows, FF_CHUNK), F32),
                        pltpu.VMEM((2, rows, FF_CHUNK), F32),
                        pltpu.VMEM((N_FF_CHUNKS, TM, FF_CHUNK), BF16)],
        compiler_params=pltpu.CompilerParams(
            dimension_semantics=("arbitrary", "arbitrary"), vmem_limit_bytes=VMEM_LIMIT),
        name="ffn",
    )(x1, x1, x1, lw["g_ffn"], lw["w_up_a"], lw["w_up_b"], lw["ffn_vec"], lw["w_down"], g_final)


def _pack_layer(p, l):
    pieces = jnp.split(p["w_in"][l], np.cumsum(IN_SIZES)[:-1].tolist(), axis=1)
    aq, ak, av, bcq, bckv, bkr, cq, ck, cv, dq, dk, dv = pieces
    z = lambda n: jnp.zeros((D_MODEL, n), F32)
    w_in = jnp.concatenate([aq, ak, av, bcq, bckv, z(64), bkr, z(32), cq, ck, cv, dq, dk, dv], axis=1)

    wq = p["b_w_q_up"][l].reshape(B_Q_RANK, B_HEADS, B_NOPE + B_ROPE)
    wq = jnp.pad(wq, ((0, 0), (0, 0), (0, 128 - B_NOPE - B_ROPE))).reshape(B_Q_RANK, B_HEADS * 128)
    wkv = p["b_w_kv_up"][l].reshape(B_KV_RANK, B_HEADS, B_NOPE + B_V)
    wk = jnp.pad(wkv[:, :, :B_NOPE], ((0, 0), (0, 0), (0, 128 - B_NOPE))).reshape(B_KV_RANK, B_HEADS * 128)
    wv = wkv[:, :, B_NOPE:].reshape(B_KV_RANK, B_HEADS * B_V)

    w_up = p["w_up"][l]
    chunked = lambda w: w.reshape(D_MODEL, N_FF_CHUNKS, FF_CHUNK).transpose(1, 0, 2)
    halves = lambda v: (v[..., :D_FF], v[..., D_FF:])
    bua, bub = halves(p["b_up"][l])
    cwa, cwb = halves(p["conv_w"][l])
    cba, cbb = halves(p["conv_b"][l])
    rows = [bua, bub, cwa[0], cwa[1], cwa[2], cwb[0], cwb[1], cwb[2], cba, cbb]
    vec = jnp.stack(rows + [jnp.zeros_like(bua)] * (16 - len(rows)), axis=0)
    vec = vec.reshape(16, N_FF_CHUNKS, FF_CHUNK).transpose(1, 0, 2)
    row = lambda v: v.reshape(1, -1)
    return {
        "g_attn": row(p["g_attn"][l]),
        "w_in": w_in.astype(BF16),
        "a_sink": row(p["a_sink"][l]),
        "b_q_norm": row(p["b_q_norm"][l]),
        "w_q_up": wq.astype(BF16),
        "b_kv_norm": row(p["b_kv_norm"][l]),
        "w_kv_up": jnp.concatenate([wk, wv], axis=1).astype(BF16),
        "c_q_norm": row(jnp.tile(p["c_q_norm"][l], 4)),
        "c_k_norm": row(jnp.tile(p["c_k_norm"][l], 2)),
        "d_lambda": tuple(row(p[n][l]) for n in ("d_lambda_q1", "d_lambda_k1", "d_lambda_q2", "d_lambda_k2")),
        "d_sub_norm": row(jnp.tile(p["d_sub_norm"][l], 2)),
        "w_out": p["w_out"][l].astype(BF16),
        "g_ffn": row(p["g_ffn"][l]),
        "w_up_a": chunked(w_up[:, :D_FF]).astype(BF16),
        "w_up_b": chunked(w_up[:, D_FF:]).astype(BF16),
        "ffn_vec": vec,
        "w_down": p["w_down"][l].reshape(N_FF_CHUNKS, FF_CHUNK, D_MODEL).astype(BF16),
    }


def _rope_tables(s_len):
    half = 16
    inv = ROPE_THETA ** (-jnp.arange(half, dtype=F32) * 2.0 / (2 * half))
    pos = jnp.arange(s_len, dtype=F32)
    rows = s_len // GRID_W
    row_pos = jnp.broadcast_to(jnp.arange(rows, dtype=F32)[:, None], (rows, GRID_W)).reshape(s_len)
    col_pos = jnp.broadcast_to(jnp.arange(GRID_W, dtype=F32)[None, :], (rows, GRID_W)).reshape(s_len)

    def cs(p):
        ang = p[:, None] * inv[None, :]
        return jnp.cos(ang), jnp.sin(ang)

    one = lambda n: jnp.ones((s_len, n), F32)
    zero = lambda n: jnp.zeros((s_len, n), F32)
    c, s = cs(pos)
    rope_b = jnp.stack([
        jnp.concatenate([one(64), c, c, one(32)], axis=1),
        jnp.concatenate([zero(64), -s, zero(16), zero(32)], axis=1),
        jnp.concatenate([zero(64), zero(16), s, zero(32)], axis=1)])
    cr, sr = cs(row_pos)
    cc, sc = cs(col_pos)
    z = zero(16)
    rope_c = jnp.stack([
        jnp.tile(jnp.concatenate([cr, cr, cc, cc], axis=1), (1, 2)),
        jnp.tile(jnp.concatenate([-sr, z, -sc, z], axis=1), (1, 2)),
        jnp.tile(jnp.concatenate([z, sr, z, sc], axis=1), (1, 2))])
    return rope_b, rope_c


def _trunk(x, layers, g_final):
    bsz, s_len, _ = x.shape
    nc = s_len // TK
    rope_b, rope_c = _rope_tables(s_len)
    depth = len(layers)
    for l, lw in enumerate(layers):
        qa, ka, va, qb, ktb, vb, qc, ktc, vc, qd, ktd, vd = _proj_call(x, lw, rope_b, rope_c)
        oa = _attn_a_call(qa, ka, va, lw["a_sink"])
        ob = _full_attn_call("b", qb, ktb, vb.reshape(bsz, nc, TK, 512))
        oc = _full_attn_call("c", qc, ktc, vc.reshape(bsz, 2, nc, TK, 128))
        lam_init = 0.8 - 0.6 * math.exp(-0.3 * l)
        od = _full_attn_call("d", qd, ktd, vd.reshape(bsz, nc, TK, 512),
                             extra=(*lw["d_lambda"], lw["d_sub_norm"]), lam_init=lam_init)
        x1 = _out_proj_call(x, oa, ob, oc, od, lw)
        x = _ffn_call(x1, lw, g_final, final_norm=(l == depth - 1))
    return x


def kernel(x_prompt, x_sample, g_attn, w_in, a_sink, b_q_norm, b_w_q_up, b_kv_norm, b_w_kv_up,
           c_q_norm, c_k_norm, d_lambda_q1, d_lambda_k1, d_lambda_q2, d_lambda_k2, d_sub_norm, w_out,
           g_ffn, w_up, b_up, conv_w, conv_b, w_down, g_final):
    p = dict(g_attn=g_attn, w_in=w_in, a_sink=a_sink, b_q_norm=b_q_norm, b_w_q_up=b_w_q_up,
             b_kv_norm=b_kv_norm, b_w_kv_up=b_w_kv_up, c_q_norm=c_q_norm, c_k_norm=c_k_norm,
             d_lambda_q1=d_lambda_q1, d_lambda_k1=d_lambda_k1, d_lambda_q2=d_lambda_q2,
             d_lambda_k2=d_lambda_k2, d_sub_norm=d_sub_norm, w_out=w_out, g_ffn=g_ffn, w_up=w_up,
             b_up=b_up, conv_w=conv_w, conv_b=conv_b, w_down=w_down)
    layers = [_pack_layer(p, l) for l in range(g_attn.shape[0])]
    gf = g_final.reshape(1, -1)
    return _trunk(x_prompt, layers, gf), _trunk(x_sample, layers, gf)
```

```python
import functools
import math

import jax
import jax.numpy as jnp
import numpy as np
from jax import lax
from jax.experimental import pallas as pl
from jax.experimental.pallas import tpu as pltpu

F32 = jnp.float32
BF16 = jnp.bfloat16

D_MODEL = 1024
GRID_W = 64
BLOCK = 128
HEAD_DIM = 64
EPS = 1e-6
ROPE_THETA = 10000.0

A_HEADS = 4
WINDOW = 128
B_HEADS = 4
B_Q_RANK = 256
B_KV_RANK = 128
B_NOPE = 64
B_ROPE = 32
B_V = 64
D_QK = 32
D_V = 64
N_ALIBI = 8
D_FF = 2816
IN_SIZES = (256, 128, 128, 256, 128, 32, 256, 128, 128, 256, 256, 256)

SLOPES = tuple(2.0 ** (-8.0 * (i + 1.0) / N_ALIBI) for i in range(N_ALIBI))
LOG2E = math.log2(math.e)


def _split3(x):
    rest, out = np.float32(x), []
    for _ in range(3):
        part = np.float32(np.asarray(rest).astype(jnp.bfloat16))
        out.append(float(part))
        rest = np.float32(rest - part)
    return tuple(out)


D_SLOPE_PARTS = tuple(_split3(SLOPES[A_HEADS + h] * LOG2E) for h in range(4))

LANES = 128
TM = 512
TK = 512
TQ = 512
TQ_D = 256
TQA = 256
TKA = 512
N_CHUNKS_A = -(-(TQA + 2 * WINDOW) // TKA)
SUBLANES = 8
FF_CHUNK = 256
N_FF_CHUNKS = D_FF // FF_CHUNK
VMEM_LIMIT = 56 * 1024 * 1024

_SEC_A, _SEC_B, _SEC_C, _SEC_D = (0, 512), (512, 1024), (1024, 1536), (1536, 2304)
_OFF_AQ, _OFF_AK, _OFF_AV = 0, 256, 384
_OFF_BCQ, _OFF_BCKV, _OFF_BKR = 0, 256, 384
_OFF_CQ, _OFF_CK, _OFF_CV = 0, 256, 384
_OFF_DQ, _OFF_DK, _OFF_DV = 0, 256, 512
IN_PACKED = 2304


def _dot(a, b):
    return jnp.dot(a, b, preferred_element_type=F32)


def _rms(x, g):
    return x * lax.rsqrt(jnp.mean(x * x, axis=-1, keepdims=True) + EPS) * g


def _rope(t, tab_ref):
    return (t * tab_ref[0] + pltpu.roll(t, LANES - 16, 1) * tab_ref[1]
            + pltpu.roll(t, 16, 1) * tab_ref[2])


def _dup_halves(t):
    lane = lax.broadcasted_iota(jnp.int32, t.shape, 1)
    sw = pltpu.roll(t, 64, 1)
    lo = lane < 64
    return jnp.where(lo, t, sw), jnp.where(lo, sw, t)


def _with_ones(t):
    lane = lax.broadcasted_iota(jnp.int32, t.shape, 1)
    lo = lane < 64
    return jnp.where(lo, t, 1.0), jnp.where(lo, pltpu.roll(t, 64, 1), 1.0)


def _head_mean_sq(t):
    w = t.shape[1]
    r = lax.broadcasted_iota(jnp.int32, (w, w), 0) // HEAD_DIM
    c = lax.broadcasted_iota(jnp.int32, (w, w), 1) // HEAD_DIM
    pool = jnp.where(r == c, 1.0 / HEAD_DIM, 0.0).astype(BF16)
    t2 = t * t
    hi = t2.astype(BF16)
    lo = (t2 - hi.astype(F32)).astype(BF16)
    return _dot(hi, pool) + _dot(lo, pool)


def _proj_kernel(x_ref, g_ref, win_ref, bqn_ref, wq_ref, bkvn_ref, wkv_ref, cqn_ref, ckn_ref,
                 ropeb_ref, ropec_ref,
                 qa_ref, ka_ref, va_ref, qb_ref, ktb_ref, vb_ref, qc_ref, ktc_ref, vc_ref,
                 qd_ref, ktd_ref, vd_ref):
    x = x_ref[0]
    xn = _rms(x, g_ref[...]).astype(BF16)
    h_all = _dot(xn, win_ref[...])
    project = lambda sec: h_all[:, sec[0]:sec[1]]

    h = project(_SEC_A)
    qa_ref[0] = (h[:, _OFF_AQ:_OFF_AQ + 256] * (HEAD_DIM ** -0.5 * LOG2E)).astype(BF16)
    for grp, kd in enumerate(_dup_halves(h[:, _OFF_AK:_OFF_AK + 128])):
        for j in range(TM // BLOCK):
            ka_ref[0, grp, j] = kd[j * BLOCK:(j + 1) * BLOCK].T.astype(BF16)
    v0, v1 = _with_ones(h[:, _OFF_AV:_OFF_AV + 128])
    va_ref[0, 0] = v0.astype(BF16)
    va_ref[0, 1] = v1.astype(BF16)

    h = project(_SEC_B)
    cq = _rms(h[:, _OFF_BCQ:_OFF_BCQ + B_Q_RANK], bqn_ref[...]).astype(BF16)
    qb = _dot(cq, wq_ref[...])
    scale_b = (B_NOPE + B_ROPE) ** -0.5 * LOG2E
    for hd in range(B_HEADS):
        t = _rope(qb[:, hd * 128:(hd + 1) * 128], ropeb_ref)
        qb_ref[0, :, hd * 128:(hd + 1) * 128] = (t * scale_b).astype(BF16)
    ckv = _rms(h[:, _OFF_BCKV:_OFF_BCKV + B_KV_RANK], bkvn_ref[...]).astype(BF16)
    kv = _dot(ckv, wkv_ref[...])
    kr = _rope(h[:, _OFF_BKR:_OFF_BKR + 128], ropeb_ref)
    for hd in range(B_HEADS):
        ktb_ref[0, hd, 0] = (kv[:, hd * 128:(hd + 1) * 128] + kr).T.astype(BF16)
    for grp in range(2):
        v0, v1 = _with_ones(kv[:, 512 + grp * 128:512 + (grp + 1) * 128])
        vb_ref[0, :, (2 * grp) * 128:(2 * grp + 1) * 128] = v0.astype(BF16)
        vb_ref[0, :, (2 * grp + 1) * 128:(2 * grp + 2) * 128] = v1.astype(BF16)

    h = project(_SEC_C)
    cqh = h[:, _OFF_CQ:_OFF_CQ + 256]
    cqh = cqh * lax.rsqrt(_head_mean_sq(cqh) + EPS) * cqn_ref[...]
    for grp in range(2):
        t = _rope(cqh[:, grp * 128:(grp + 1) * 128], ropec_ref)
        qc_ref[0, :, grp * 128:(grp + 1) * 128] = (t * (HEAD_DIM ** -0.5 * LOG2E)).astype(BF16)
    ckh = h[:, _OFF_CK:_OFF_CK + 128]
    ckh = ckh * lax.rsqrt(_head_mean_sq(ckh) + EPS) * ckn_ref[...]
    ckh = _rope(ckh, ropec_ref)
    k0, k1 = _dup_halves(ckh)
    ktc_ref[0, 0, 0] = k0.T.astype(BF16)
    ktc_ref[0, 1, 0] = k1.T.astype(BF16)
    v0, v1 = _with_ones(h[:, _OFF_CV:_OFF_CV + 128])
    vc_ref[0, 0] = v0.astype(BF16)
    vc_ref[0, 1] = v1.astype(BF16)

    h = project(_SEC_D)
    lane = lax.broadcasted_iota(jnp.int32, (TM, LANES), 1)
    for grp in range(2):
        qg = h[:, _OFF_DQ + grp * 128:_OFF_DQ + (grp + 1) * 128] * (D_QK ** -0.5 * LOG2E)
        for u in range(4):
            t = qg if u == 0 else pltpu.roll(qg, LANES - D_QK * u, 1)
            qd_ref[0, :, (4 * grp + u) * LANES:(4 * grp + u + 1) * LANES] = (
                jnp.where(lane < D_QK, t, 0.0).astype(BF16))
    for grp in range(2):
        ktd_ref[0, grp, 0] = h[:, _OFF_DK + grp * 128:_OFF_DK + (grp + 1) * 128].T.astype(BF16)
    for grp in range(2):
        v0, v1 = _with_ones(h[:, _OFF_DV + grp * 128:_OFF_DV + (grp + 1) * 128])
        vd_ref[0, :, (2 * grp) * 128:(2 * grp + 1) * 128] = v0.astype(BF16)
        vd_ref[0, :, (2 * grp + 1) * 128:(2 * grp + 2) * 128] = v1.astype(BF16)


def _const_spec(shape):
    nd = len(shape)
    return pl.BlockSpec(shape, lambda *_: (0,) * nd)


def _per_batch_spec(a):
    return pl.BlockSpec((1,) + a.shape[1:], lambda b, i: (b,) + (0,) * (a.ndim - 1))


def _proj_call(x, lw, rope_b, rope_c):
    bsz, s_len, _ = x.shape
    nt = s_len // TM
    tok = lambda w: pl.BlockSpec((1, TM, w), lambda b, i: (b, i, 0))
    dup = pl.BlockSpec((1, 2, TM, 128), lambda b, i: (b, 0, i, 0))
    kt = lambda n: pl.BlockSpec((1, n, 1, 128, TK), lambda b, i: (b, 0, i, 0, 0))
    sds = jax.ShapeDtypeStruct
    out_shape = (
        sds((bsz, s_len, 256), BF16), sds((bsz, 2, s_len // BLOCK, 128, BLOCK), BF16),
        sds((bsz, 2, s_len, 128), BF16),
        sds((bsz, s_len, 512), BF16), sds((bsz, 4, nt, 128, TK), BF16), sds((bsz, s_len, 512), BF16),
        sds((bsz, s_len, 256), BF16), sds((bsz, 2, nt, 128, TK), BF16), sds((bsz, 2, s_len, 128), BF16),
        sds((bsz, s_len, 1024), BF16), sds((bsz, 2, nt, 128, TK), BF16), sds((bsz, s_len, 512), BF16),
    )
    kta = pl.BlockSpec((1, 2, TM // BLOCK, 128, BLOCK), lambda b, i: (b, 0, i, 0, 0))
    out_specs = (tok(256), kta, dup, tok(512), kt(4), tok(512),
                 tok(256), kt(2), dup, tok(1024), kt(2), tok(512))
    rope_spec = pl.BlockSpec((3, TM, 128), lambda b, i: (0, i, 0))
    in_specs = [
        tok(D_MODEL), _const_spec((1, D_MODEL)), _const_spec((D_MODEL, IN_PACKED)),
        _const_spec((1, B_Q_RANK)), _const_spec((B_Q_RANK, 512)),
        _const_spec((1, B_KV_RANK)), _const_spec((B_KV_RANK, 768)),
        _const_spec((1, 256)), _const_spec((1, 128)),
        rope_spec, rope_spec,
    ]
    return pl.pallas_call(
        _proj_kernel,
        grid=(bsz, nt),
        in_specs=in_specs,
        out_specs=out_specs,
        out_shape=out_shape,
        compiler_params=pltpu.CompilerParams(
            dimension_semantics=("arbitrary", "arbitrary"), vmem_limit_bytes=VMEM_LIMIT),
        name="in_proj",
    )(x, lw["g_attn"], lw["w_in"], lw["b_q_norm"], lw["w_q_up"], lw["b_kv_norm"], lw["w_kv_up"],
      lw["c_q_norm"], lw["c_k_norm"], rope_b, rope_c)


def _fold(op, s):
    out = s[:, 0:LANES]
    for j in range(1, s.shape[1] // LANES):
        out = op(out, s[:, j * LANES:(j + 1) * LANES])
    return out


def _attend(streams, n_chunks, s_scr, add_at=None, m_floor=None):
    n = len(streams)
    rows = s_scr.shape[2]
    m_run = [jnp.full((rows, LANES), -jnp.inf, F32)] * n
    acc = [jnp.zeros((rows, LANES), F32)] * n
    m = [None] * n
    for k in range(n + 1):
        for c in range(n_chunks):
            if k < n:
                qm, kt_at, _ = streams[k]
                s = _dot(qm(c) if callable(qm) else qm, kt_at(c))
                bias = None if add_at is None else add_at(k, c)
                if bias is not None:
                    s = s + bias
                s_scr[k % 2, c] = s
                m_run[k] = jnp.maximum(m_run[k], _fold(jnp.maximum, s))
            if k >= 1:
                j = k - 1
                p = jnp.exp2((s_scr[j % 2, c] - m[j]).astype(BF16))
                acc[j] = acc[j] + _dot(p, streams[j][2](c))
        if k < n:
            m[k] = jnp.max(m_run[k], axis=-1, keepdims=True)
            if m_floor is not None:
                m[k] = jnp.maximum(m[k], m_floor[k])
    return acc, m


def _normalised(acc):
    return acc / acc[:, 64:65]


def _pair_lanes(o0, o1):
    lane = lax.broadcasted_iota(jnp.int32, o0.shape, 1)
    return jnp.where(lane < 64, o0, pltpu.roll(o1, 64, 1))


def _half_masks(q):
    lane = lax.broadcasted_iota(jnp.int32, q.shape, 1)
    zero = jnp.zeros_like(q)
    return jnp.where(lane < 64, q, zero), jnp.where(lane < 64, zero, q)


def _attn_b_kernel(q_ref, kt_ref, v_ref, o_ref, s_scr, *, n_chunks):
    streams = [(q_ref[0, :, h * 128:(h + 1) * 128], lambda c, h=h: kt_ref[0, h, c],
                lambda c, h=h: v_ref[0, c, :, h * 128:(h + 1) * 128]) for h in range(B_HEADS)]
    acc, _ = _attend(streams, n_chunks, s_scr)
    for g in range(2):
        o_ref[0, :, g * 128:(g + 1) * 128] = _pair_lanes(
            _normalised(acc[2 * g]), _normalised(acc[2 * g + 1])).astype(BF16)


def _attn_c_kernel(q_ref, kt_ref, v_ref, o_ref, s_scr, *, n_chunks):
    streams = []
    for g in range(2):
        for qm in _half_masks(q_ref[0, :, g * 128:(g + 1) * 128]):
            streams.append((qm, lambda c, g=g: kt_ref[0, g, c], lambda c, g=g: v_ref[0, g, c]))
    acc, _ = _attend(streams, n_chunks, s_scr)
    for g in range(2):
        o_ref[0, :, g * 128:(g + 1) * 128] = _pair_lanes(
            _normalised(acc[2 * g]), _normalised(acc[2 * g + 1])).astype(BF16)


def _attn_d_kernel(q_ref, kt_ref, v_ref, lq1_ref, lk1_ref, lq2_ref, lk2_ref, sn_ref, o_ref, s_scr,
                   bias_scr, *, n_chunks, lam_init):
    tq = q_ref.shape[1]
    n_heads = q_ref.shape[2] // (2 * LANES)
    q0 = pl.program_id(1) * tq
    diag = q0 // TK
    lam = (jnp.exp(jnp.sum(lq1_ref[...] * lk1_ref[...], axis=-1, keepdims=True))
           - jnp.exp(jnp.sum(lq2_ref[...] * lk2_ref[...], axis=-1, keepdims=True)) + lam_init)
    lane = lax.broadcasted_iota(jnp.int32, (tq, LANES), 1)
    row = lax.broadcasted_iota(jnp.int32, (tq, LANES), 0).astype(F32)
    parts = D_SLOPE_PARTS

    f = lane - D_QK
    third = lambda idx, p: jnp.where(idx % 3 == 0, p[0], jnp.where(idx % 3 == 1, p[1], p[2]))
    q_feat, k_feat, sigma = [], [], []
    kr = lax.broadcasted_iota(jnp.int32, (LANES - D_QK, TK), 0)
    kc = lax.broadcasted_iota(jnp.int32, (LANES - D_QK, TK), 1)
    for r in range(n_heads):
        left = jnp.where((f >= 0) & (f < 6), third(f, parts[r]), jnp.where((f >= 6) & (f < 9), -row, 0.0))
        q_feat.append((left.astype(BF16), (-left).astype(BF16)))
        sig_rows = third(kr, parts[r])
        k_feat.append(jnp.where(kr < 3, (kc % 256).astype(F32),
                      jnp.where(kr < 6, (kc // 256 * 256).astype(F32),
                      jnp.where(kr < 9, sig_rows,
                      jnp.where(kr < 12, 256.0 * sig_rows, 0.0)))).astype(BF16))
        sigma.append(parts[r][0] + parts[r][1] + parts[r][2])
    off_lanes = (f >= 9) & (f < 12)

    def chunk_of(d):
        wrapped = diag + d >= n_chunks
        return jnp.where(wrapped, diag + d - n_chunks, diag + d), wrapped

    def q_at(u, d):
        qm = q_ref[0, :, u * LANES:(u + 1) * LANES]
        if d == 0:
            return qm
        c, is_left = chunk_of(d)
        off = ((q0 - c * TK) // 256).astype(F32)
        feat = jnp.where(off_lanes, jnp.where(is_left, -off, off).astype(BF16),
                         jnp.where(is_left, q_feat[u // 2][0], q_feat[u // 2][1]))
        return jnp.where(lane < D_QK, qm, feat)

    def kt_at(u, d):
        c, _ = chunk_of(d)
        return jnp.concatenate(
            [kt_ref[0, u // 4, c, (u % 4) * D_QK:(u % 4 + 1) * D_QK, :], k_feat[u // 2]], axis=0)

    def v_at(u, d):
        c, _ = chunk_of(d)
        return v_ref[0, c, :, (u // 2) * LANES:(u // 2 + 1) * LANES]

    def add_at(k, d):
        if d != 0:
            return None
        if k % 2 == 1:
            return bias_scr[0]
        rel = (lax.broadcasted_iota(jnp.int32, (tq, TK), 0)
               - lax.broadcasted_iota(jnp.int32, (tq, TK), 1)).astype(F32)
        bias = -sigma[k // 2] * jnp.abs(rel + (q0 - diag * TK).astype(F32))
        bias_scr[0] = bias
        return bias

    streams = [(functools.partial(q_at, u), functools.partial(kt_at, u), functools.partial(v_at, u))
               for u in range(2 * n_heads)]
    acc, _ = _attend(streams, n_chunks, s_scr, add_at)
    outs = []
    for r in range(n_heads):
        o = _normalised(acc[2 * r]) - lam * _normalised(acc[2 * r + 1])
        msq = jnp.sum(jnp.where(lane < 64, o * o, 0.0), axis=-1, keepdims=True) * (1.0 / D_V)
        outs.append(o * lax.rsqrt(msq + EPS) * sn_ref[...] * (1.0 - lam_init))
    for g in range(n_heads // 2):
        o_ref[0, :, g * LANES:(g + 1) * LANES] = _pair_lanes(outs[2 * g], outs[2 * g + 1]).astype(BF16)


def _full_attn_call(kind, q, kt, v, extra=(), lam_init=0.0):
    bsz, s_len = q.shape[0], q.shape[1]
    tq = TQ_D if kind == "d" else TQ
    nq, nc = s_len // tq, s_len // TK
    tok = lambda w: pl.BlockSpec((1, tq, w), lambda b, i: (b, i, 0))
    scratch = [pltpu.VMEM((2, nc, tq, TK), F32)]
    if kind == "b":
        body = functools.partial(_attn_b_kernel, n_chunks=nc)
        in_specs = [tok(512), _per_batch_spec(kt), _per_batch_spec(v)]
    elif kind == "c":
        body = functools.partial(_attn_c_kernel, n_chunks=nc)
        in_specs = [tok(256), _per_batch_spec(kt), _per_batch_spec(v)]
    else:
        body = functools.partial(_attn_d_kernel, n_chunks=nc, lam_init=lam_init)
        in_specs = [tok(1024), _per_batch_spec(kt), _per_batch_spec(v),
                    _const_spec((1, D_QK)), _const_spec((1, D_QK)),
                    _const_spec((1, D_QK)), _const_spec((1, D_QK)),
                    _const_spec((1, 128))]
        scratch.append(pltpu.VMEM((1, tq, TK), F32))
    return pl.pallas_call(
        body,
        grid=(bsz, nq),
        in_specs=in_specs,
        out_specs=tok(256),
        out_shape=jax.ShapeDtypeStruct((bsz, s_len, 256), BF16),
        scratch_shapes=scratch,
        compiler_params=pltpu.CompilerParams(
            dimension_semantics=("arbitrary", "arbitrary"), vmem_limit_bytes=VMEM_LIMIT),
        name="attn_" + kind,
    )(q, kt, v, *extra)


def _attn_a_kernel(q_ref, kt_ref, v_ref, sink_ref, o_ref, s_scr, *, s_len):
    tile = pl.program_id(1)
    q0 = tile * TQA
    n_blocks = s_len // BLOCK
    per_chunk = TKA // BLOCK
    first = tile * (TQA // BLOCK) - 1

    def key_blocks(c):
        return [jnp.clip(first + per_chunk * c + j, 0, n_blocks - 1) for j in range(per_chunk)]

    def kt_at(g, c):
        return jnp.concatenate([kt_ref[0, g, i] for i in key_blocks(c)], axis=1)

    def v_at(g, c):
        return jnp.concatenate(
            [v_ref[0, g, pl.ds(pl.multiple_of(i * BLOCK, BLOCK), BLOCK), :] for i in key_blocks(c)], axis=0)

    neg_dist = []
    for c in range(N_CHUNKS_A):
        col = lax.broadcasted_iota(jnp.int32, (TQA, TKA), 1)
        diff = lax.broadcasted_iota(jnp.int32, (TQA, TKA), 0) - col + (BLOCK - TKA * c)
        kpos = q0 - BLOCK + TKA * c + col
        valid = (jnp.abs(diff) <= WINDOW) & (kpos >= 0) & (kpos < s_len)
        neg_dist.append(jnp.where(valid, -jnp.abs(diff).astype(F32), -jnp.inf))

    streams = []
    for g in range(2):
        for qm in _half_masks(q_ref[0, :, g * 128:(g + 1) * 128]):
            streams.append((qm, functools.partial(kt_at, g), functools.partial(v_at, g)))
    sinks = [sink_ref[:, h:h + 1] * LOG2E for h in range(A_HEADS)]
    acc, m = _attend(streams, N_CHUNKS_A, s_scr,
                     add_at=lambda k, c: (SLOPES[k] * LOG2E) * neg_dist[c], m_floor=sinks)
    outs = []
    for h in range(A_HEADS):
        den = acc[h] + jnp.exp2(sinks[h] - m[h])
        outs.append(acc[h] / den[:, 64:65])
    for g in range(2):
        o_ref[0, :, g * 128:(g + 1) * 128] = _pair_lanes(outs[2 * g], outs[2 * g + 1]).astype(BF16)


def _attn_a_call(q, kt, v, sink):
    bsz, s_len = q.shape[0], q.shape[1]
    tok = pl.BlockSpec((1, TQA, 256), lambda b, i: (b, i, 0))
    return pl.pallas_call(
        functools.partial(_attn_a_kernel, s_len=s_len),
        grid=(bsz, s_len // TQA),
        in_specs=[tok, _per_batch_spec(kt), _per_batch_spec(v), _const_spec((1, A_HEADS))],
        out_specs=tok,
        out_shape=jax.ShapeDtypeStruct((bsz, s_len, 256), BF16),
        scratch_shapes=[pltpu.VMEM((2, N_CHUNKS_A, TQA, TKA), F32)],
        compiler_params=pltpu.CompilerParams(
            dimension_semantics=("arbitrary", "arbitrary"), vmem_limit_bytes=VMEM_LIMIT),
        name="attn_a",
    )(q, kt, v, sink)


def _out_proj_kernel(x_ref, oa_ref, ob_ref, oc_ref, od_ref, wo_ref, x1_ref):
    x1 = x_ref[0]
    for j, o_ref in enumerate((oa_ref, ob_ref, oc_ref, od_ref)):
        x1 = x1 + _dot(o_ref[0], wo_ref[j * 256:(j + 1) * 256, :])
    x1_ref[0] = x1


def _out_proj_call(x, oa, ob, oc, od, lw):
    bsz, s_len, _ = x.shape
    tok = lambda w: pl.BlockSpec((1, TM, w), lambda b, i: (b, i, 0))
    return pl.pallas_call(
        _out_proj_kernel,
        grid=(bsz, s_len // TM),
        in_specs=[tok(D_MODEL), tok(256), tok(256), tok(256), tok(256),
                  _const_spec((D_MODEL, D_MODEL))],
        out_specs=tok(D_MODEL),
        out_shape=jax.ShapeDtypeStruct((bsz, s_len, D_MODEL), F32),
        compiler_params=pltpu.CompilerParams(
            dimension_semantics=("arbitrary", "arbitrary"), vmem_limit_bytes=VMEM_LIMIT),
        name="out_proj",
    )(x, oa, ob, oc, od, lw["w_out"])


def _transpose8(blocks):
    sub = lax.broadcasted_iota(jnp.int32, blocks[0].shape, 0)
    for d in (4, 2, 1):
        keep = (sub & d) == 0
        nxt = list(blocks)
        for i in range(SUBLANES):
            if i & d == 0:
                lo, hi = blocks[i], blocks[i + d]
                nxt[i] = jnp.where(keep, lo, pltpu.roll(hi, d, 0))
                nxt[i + d] = jnp.where(keep, pltpu.roll(lo, SUBLANES - d, 0), hi)
        blocks = nxt
    return blocks


def _ffn_kernel(x1_ref, prev_ref, next_ref, g_ref, wua_ref, wub_ref, vec_ref, wd_ref, gf_ref,
                o_ref, xf_ref, xb_ref, h0_ref, h1_ref, act_ref, *, final_norm):
    first = pl.program_id(1) == 0
    last = pl.program_id(1) == pl.num_programs(1) - 1
    nv = TM // SUBLANES

    sub = lax.broadcasted_iota(jnp.int32, (SUBLANES, D_MODEL), 0)
    for jb in range(nv // SUBLANES):
        blocks = [x1_ref[0, s * nv + jb * SUBLANES:s * nv + (jb + 1) * SUBLANES, :] for s in range(SUBLANES)]
        for jj, rows in enumerate(_transpose8(blocks)):
            j = jb * SUBLANES + jj
            xf_ref[SUBLANES * (j + 1):SUBLANES * (j + 2), :] = rows
            if j == nv - 1:
                xf_ref[0:SUBLANES, :] = jnp.where(
                    sub == 0, prev_ref[0, SUBLANES - 1:SUBLANES, :], pltpu.roll(rows, 1, 0))
            if j == 0:
                xf_ref[TM + SUBLANES:TM + 2 * SUBLANES, :] = jnp.where(
                    sub == SUBLANES - 1, next_ref[0, 0:1, :], pltpu.roll(rows, SUBLANES - 1, 0))
    xb_ref[...] = _rms(xf_ref[...], g_ref[...]).astype(BF16)

    def up(c, h_ref):
        xe = xb_ref[...]
        h_ref[0] = _dot(xe, wua_ref[c])
        h_ref[1] = _dot(xe, wub_ref[c])

    sub_h = lax.broadcasted_iota(jnp.int32, (SUBLANES, FF_CHUNK), 0)

    def conv(h_ref, half, bias, w0, w1, w2, cb):
        top = h_ref[half, 0:SUBLANES, :]
        h_ref[half, 0:SUBLANES, :] = jnp.where(first & (sub_h == 0), -bias, top)
        bot = h_ref[half, TM + SUBLANES:TM + 2 * SUBLANES, :]
        h_ref[half, TM + SUBLANES:TM + 2 * SUBLANES, :] = jnp.where(last & (sub_h == SUBLANES - 1), -bias, bot)
        return (w0 * h_ref[half, 0:TM, :] + w1 * h_ref[half, SUBLANES:SUBLANES + TM, :]
                + w2 * h_ref[half, 2 * SUBLANES:2 * SUBLANES + TM, :] + (bias * (w0 + w1 + w2) + cb))

    def gate(c, h_ref):
        vec = vec_ref[c]
        ca = conv(h_ref, 0, vec[0:1], vec[2:3], vec[3:4], vec[4:5], vec[8:9])
        cb = conv(h_ref, 1, vec[1:2], vec[5:6], vec[6:7], vec[7:8], vec[9:10])
        act_ref[c] = ((ca / (1.0 + jnp.exp(-ca))) * cb).astype(BF16)

    up(0, h0_ref)

    def pair(j, carry):
        up(2 * j + 1, h1_ref)
        gate(2 * j, h0_ref)
        up(2 * j + 2, h0_ref)
        gate(2 * j + 1, h1_ref)
        return carry

    lax.fori_loop(0, (N_FF_CHUNKS - 1) // 2, pair, 0, unroll=True)
    gate(N_FF_CHUNKS - 1, h0_ref)

    for n in range(D_MODEL // FF_CHUNK):
        cols = slice(n * FF_CHUNK, (n + 1) * FF_CHUNK)
        y = _dot(act_ref[0], wd_ref[0, :, cols])
        for c in range(1, N_FF_CHUNKS):
            y = y + _dot(act_ref[c], wd_ref[c, :, cols])
        xf_ref[0:TM, cols] = y
        for jb in range(nv // SUBLANES):
            groups = [xf_ref[(jb * SUBLANES + jj) * SUBLANES:(jb * SUBLANES + jj + 1) * SUBLANES, cols]
                      for jj in range(SUBLANES)]
            for s, blk in enumerate(_transpose8(groups)):
                rows = slice(s * nv + jb * SUBLANES, s * nv + (jb + 1) * SUBLANES)
                o_ref[0, rows, cols] = x1_ref[0, rows, cols] + blk
    if final_norm:
        o_ref[0] = _rms(o_ref[0], gf_ref[...])


def _ffn_call(x1, lw, g_final, final_norm):
    bsz, s_len, _ = x1.shape
    per_tile = TM // SUBLANES
    last = s_len // SUBLANES - 1
    tok = pl.BlockSpec((1, TM, D_MODEL), lambda b, i: (b, i, 0))
    prev = pl.BlockSpec((1, SUBLANES, D_MODEL), lambda b, i: (b, jnp.maximum(i * per_tile - 1, 0), 0))
    nxt = pl.BlockSpec((1, SUBLANES, D_MODEL),
                       lambda b, i: (b, jnp.minimum((i + 1) * per_tile, last), 0))
    rows = TM + 2 * SUBLANES
    return pl.pallas_call(
        functools.partial(_ffn_kernel, final_norm=final_norm),
        grid=(bsz, s_len // TM),
        in_specs=[tok, prev, nxt, _const_spec((1, D_MODEL)),
                  _const_spec((N_FF_CHUNKS, D_MODEL, FF_CHUNK)),
                  _const_spec((N_FF_CHUNKS, D_MODEL, FF_CHUNK)),
                  _const_spec((N_FF_CHUNKS, 16, FF_CHUNK)),
                  _const_spec((N_FF_CHUNKS, FF_CHUNK, D_MODEL)),
                  _const_spec((1, D_MODEL))],
        out_specs=tok,
        out_shape=jax.ShapeDtypeStruct((bsz, s_len, D_MODEL), F32),
        scratch_shapes=[pltpu.VMEM((rows, D_MODEL), F32),
                        pltpu.VMEM((rows, D_MODEL), BF16),
                        pltpu.VMEM((2, rows, FF_CHUNK), F32),
                        pltpu.VMEM((2, rows, FF_CHUNK), F32),
                        pltpu.VMEM((N_FF_CHUNKS, TM, FF_CHUNK), BF16)],
        compiler_params=pltpu.CompilerParams(
            dimension_semantics=("arbitrary", "arbitrary"), vmem_limit_bytes=VMEM_LIMIT),
        name="ffn",
    )(x1, x1, x1, lw["g_ffn"], lw["w_up_a"], lw["w_up_b"], lw["ffn_vec"], lw["w_down"], g_final)


def _pack_layer(p, l):
    pieces = jnp.split(p["w_in"][l], np.cumsum(IN_SIZES)[:-1].tolist(), axis=1)
    aq, ak, av, bcq, bckv, bkr, cq, ck, cv, dq, dk, dv = pieces
    z = lambda n: jnp.zeros((D_MODEL, n), F32)
    w_in = jnp.concatenate([aq, ak, av, bcq, bckv, z(64), bkr, z(32), cq, ck, cv, dq, dk, dv], axis=1)

    wq = p["b_w_q_up"][l].reshape(B_Q_RANK, B_HEADS, B_NOPE + B_ROPE)
    wq = jnp.pad(wq, ((0, 0), (0, 0), (0, 128 - B_NOPE - B_ROPE))).reshape(B_Q_RANK, B_HEADS * 128)
    wkv = p["b_w_kv_up"][l].reshape(B_KV_RANK, B_HEADS, B_NOPE + B_V)
    wk = jnp.pad(wkv[:, :, :B_NOPE], ((0, 0), (0, 0), (0, 128 - B_NOPE))).reshape(B_KV_RANK, B_HEADS * 128)
    wv = wkv[:, :, B_NOPE:].reshape(B_KV_RANK, B_HEADS * B_V)

    w_up = p["w_up"][l]
    chunked = lambda w: w.reshape(D_MODEL, N_FF_CHUNKS, FF_CHUNK).transpose(1, 0, 2)
    halves = lambda v: (v[..., :D_FF], v[..., D_FF:])
    bua, bub = halves(p["b_up"][l])
    cwa, cwb = halves(p["conv_w"][l])
    cba, cbb = halves(p["conv_b"][l])
    rows = [bua, bub, cwa[0], cwa[1], cwa[2], cwb[0], cwb[1], cwb[2], cba, cbb]
    vec = jnp.stack(rows + [jnp.zeros_like(bua)] * (16 - len(rows)), axis=0)
    vec = vec.reshape(16, N_FF_CHUNKS, FF_CHUNK).transpose(1, 0, 2)
    row = lambda v: v.reshape(1, -1)
    return {
        "g_attn": row(p["g_attn"][l]),
        "w_in": w_in.astype(BF16),
        "a_sink": row(p["a_sink"][l]),
        "b_q_norm": row(p["b_q_norm"][l]),
        "w_q_up": wq.astype(BF16),
        "b_kv_norm": row(p["b_kv_norm"][l]),
        "w_kv_up": jnp.concatenate([wk, wv], axis=1).astype(BF16),
        "c_q_norm": row(jnp.tile(p["c_q_norm"][l], 4)),
        "c_k_norm": row(jnp.tile(p["c_k_norm"][l], 2)),
        "d_lambda": tuple(row(p[n][l]) for n in ("d_lambda_q1", "d_lambda_k1", "d_lambda_q2", "d_lambda_k2")),
        "d_sub_norm": row(jnp.tile(p["d_sub_norm"][l], 2)),
        "w_out": p["w_out"][l].astype(BF16),
        "g_ffn": row(p["g_ffn"][l]),
        "w_up_a": chunked(w_up[:, :D_FF]).astype(BF16),
        "w_up_b": chunked(w_up[:, D_FF:]).astype(BF16),
        "ffn_vec": vec,
        "w_down": p["w_down"][l].reshape(N_FF_CHUNKS, FF_CHUNK, D_MODEL).astype(BF16),
    }


def _rope_tables(s_len):
    half = 16
    inv = ROPE_THETA ** (-jnp.arange(half, dtype=F32) * 2.0 / (2 * half))
    pos = jnp.arange(s_len, dtype=F32)
    rows = s_len // GRID_W
    row_pos = jnp.broadcast_to(jnp.arange(rows, dtype=F32)[:, None], (rows, GRID_W)).reshape(s_len)
    col_pos = jnp.broadcast_to(jnp.arange(GRID_W, dtype=F32)[None, :], (rows, GRID_W)).reshape(s_len)

    def cs(p):
        ang = p[:, None] * inv[None, :]
        return jnp.cos(ang), jnp.sin(ang)

    one = lambda n: jnp.ones((s_len, n), F32)
    zero = lambda n: jnp.zeros((s_len, n), F32)
    c, s = cs(pos)
    rope_b = jnp.stack([
        jnp.concatenate([one(64), c, c, one(32)], axis=1),
        jnp.concatenate([zero(64), -s, zero(16), zero(32)], axis=1),
        jnp.concatenate([zero(64), zero(16), s, zero(32)], axis=1)])
    cr, sr = cs(row_pos)
    cc, sc = cs(col_pos)
    z = zero(16)
    rope_c = jnp.stack([
        jnp.tile(jnp.concatenate([cr, cr, cc, cc], axis=1), (1, 2)),
        jnp.tile(jnp.concatenate([-sr, z, -sc, z], axis=1), (1, 2)),
        jnp.tile(jnp.concatenate([z, sr, z, sc], axis=1), (1, 2))])
    return rope_b, rope_c


def _trunk(x, layers, g_final):
    bsz, s_len, _ = x.shape
    nc = s_len // TK
    rope_b, rope_c = _rope_tables(s_len)
    depth = len(layers)
    for l, lw in enumerate(layers):
        qa, ka, va, qb, ktb, vb, qc, ktc, vc, qd, ktd, vd = _proj_call(x, lw, rope_b, rope_c)
        oa = _attn_a_call(qa, ka, va, lw["a_sink"])
        ob = _full_attn_call("b", qb, ktb, vb.reshape(bsz, nc, TK, 512))
        oc = _full_attn_call("c", qc, ktc, vc.reshape(bsz, 2, nc, TK, 128))
        lam_init = 0.8 - 0.6 * math.exp(-0.3 * l)
        od = _full_attn_call("d", qd, ktd, vd.reshape(bsz, nc, TK, 512),
                             extra=(*lw["d_lambda"], lw["d_sub_norm"]), lam_init=lam_init)
        x1 = _out_proj_call(x, oa, ob, oc, od, lw)
        x = _ffn_call(x1, lw, g_final, final_norm=(l == depth - 1))
    return x


def kernel(x_prompt, x_sample, g_attn, w_in, a_sink, b_q_norm, b_w_q_up, b_kv_norm, b_w_kv_up,
           c_q_norm, c_k_norm, d_lambda_q1, d_lambda_k1, d_lambda_q2, d_lambda_k2, d_sub_norm, w_out,
           g_ffn, w_up, b_up, conv_w, conv_b, w_down, g_final):
    p = dict(g_attn=g_attn, w_in=w_in, a_sink=a_sink, b_q_norm=b_q_norm, b_w_q_up=b_w_q_up,
             b_kv_norm=b_kv_norm, b_w_kv_up=b_w_kv_up, c_q_norm=c_q_norm, c_k_norm=c_k_norm,
             d_lambda_q1=d_lambda_q1, d_lambda_k1=d_lambda_k1, d_lambda_q2=d_lambda_q2,
             d_lambda_k2=d_lambda_k2, d_sub_norm=d_sub_norm, w_out=w_out, g_ffn=g_ffn, w_up=w_up,
             b_up=b_up, conv_w=conv_w, conv_b=conv_b, w_down=w_down)
    layers = [_pack_layer(p, l) for l in range(g_attn.shape[0])]
    gf = g_final.reshape(1, -1)
    return _trunk(x_prompt, layers, gf), _trunk(x_sample, layers, gf)
```

```python
import functools
import math

import jax
import jax.numpy as jnp
import numpy as np
from jax import lax
from jax.experimental import pallas as pl
from jax.experimental.pallas import tpu as pltpu

F32 = jnp.float32
BF16 = jnp.bfloat16

D_MODEL = 1024
GRID_W = 64
BLOCK = 128
HEAD_DIM = 64
EPS = 1e-6
ROPE_THETA = 10000.0

A_HEADS = 4
WINDOW = 128
B_HEADS = 4
B_Q_RANK = 256
B_KV_RANK = 128
B_NOPE = 64
B_ROPE = 32
B_V = 64
D_QK = 32
D_V = 64
N_ALIBI = 8
D_FF = 2816
IN_SIZES = (256, 128, 128, 256, 128, 32, 256, 128, 128, 256, 256, 256)

SLOPES = tuple(2.0 ** (-8.0 * (i + 1.0) / N_ALIBI) for i in range(N_ALIBI))
LOG2E = math.log2(math.e)


def _split3(x):
    rest, out = np.float32(x), []
    for _ in range(3):
        part = np.float32(np.asarray(rest).astype(jnp.bfloat16))
        out.append(float(part))
        rest = np.float32(rest - part)
    return tuple(out)


D_SLOPE_PARTS = tuple(_split3(SLOPES[A_HEADS + h] * LOG2E) for h in range(4))

LANES = 128
TM = 512
TK = 512
TQ = 512
TQ_D = 512
TQA = 256
SUB_A = 2
TKA = 512
N_CHUNKS_A = -(-(TQA + 2 * WINDOW) // TKA)
SUBLANES = 8
FF_CHUNK = 256
N_FF_CHUNKS = D_FF // FF_CHUNK
VMEM_LIMIT = 56 * 1024 * 1024

_SEC_A, _SEC_B, _SEC_C, _SEC_D = (0, 512), (512, 1024), (1024, 1536), (1536, 2304)
_OFF_AQ, _OFF_AK, _OFF_AV = 0, 256, 384
_OFF_BCQ, _OFF_BCKV, _OFF_BKR = 0, 256, 384
_OFF_CQ, _OFF_CK, _OFF_CV = 0, 256, 384
_OFF_DQ, _OFF_DK, _OFF_DV = 0, 256, 512
IN_PACKED = 2304


def _dot(a, b):
    return jnp.dot(a, b, preferred_element_type=F32)


def _rms(x, g):
    return x * lax.rsqrt(jnp.mean(x * x, axis=-1, keepdims=True) + EPS) * g


def _rope(t, tab_ref):
    return (t * tab_ref[0] + pltpu.roll(t, LANES - 16, 1) * tab_ref[1]
            + pltpu.roll(t, 16, 1) * tab_ref[2])


def _dup_halves(t):
    lane = lax.broadcasted_iota(jnp.int32, t.shape, 1)
    sw = pltpu.roll(t, 64, 1)
    lo = lane < 64
    return jnp.where(lo, t, sw), jnp.where(lo, sw, t)


def _with_ones(t):
    lane = lax.broadcasted_iota(jnp.int32, t.shape, 1)
    lo = lane < 64
    return jnp.where(lo, t, 1.0), jnp.where(lo, pltpu.roll(t, 64, 1), 1.0)


def _head_mean_sq(t):
    w = t.shape[1]
    r = lax.broadcasted_iota(jnp.int32, (w, w), 0) // HEAD_DIM
    c = lax.broadcasted_iota(jnp.int32, (w, w), 1) // HEAD_DIM
    pool = jnp.where(r == c, 1.0 / HEAD_DIM, 0.0).astype(BF16)
    t2 = t * t
    hi = t2.astype(BF16)
    lo = (t2 - hi.astype(F32)).astype(BF16)
    return _dot(hi, pool) + _dot(lo, pool)


def _proj_kernel(x_ref, g_ref, win_ref, bqn_ref, wq_ref, bkvn_ref, wkv_ref, cqn_ref, ckn_ref,
                 ropeb_ref, ropec_ref,
                 qa_ref, ka_ref, va_ref, qb_ref, ktb_ref, vb_ref, qc_ref, ktc_ref, vc_ref,
                 qd_ref, ktd_ref, vd_ref):
    x = x_ref[0]
    xn = _rms(x, g_ref[...]).astype(BF16)
    h_all = _dot(xn, win_ref[...])
    project = lambda sec: h_all[:, sec[0]:sec[1]]

    h = project(_SEC_A)
    qa_ref[0] = (h[:, _OFF_AQ:_OFF_AQ + 256] * (HEAD_DIM ** -0.5 * LOG2E)).astype(BF16)
    for grp, kd in enumerate(_dup_halves(h[:, _OFF_AK:_OFF_AK + 128])):
        for j in range(TM // BLOCK):
            ka_ref[0, grp, j] = kd[j * BLOCK:(j + 1) * BLOCK].T.astype(BF16)
    v0, v1 = _with_ones(h[:, _OFF_AV:_OFF_AV + 128])
    va_ref[0, 0] = v0.astype(BF16)
    va_ref[0, 1] = v1.astype(BF16)

    h = project(_SEC_B)
    cq = _rms(h[:, _OFF_BCQ:_OFF_BCQ + B_Q_RANK], bqn_ref[...]).astype(BF16)
    qb = _dot(cq, wq_ref[...])
    scale_b = (B_NOPE + B_ROPE) ** -0.5 * LOG2E
    for hd in range(B_HEADS):
        t = _rope(qb[:, hd * 128:(hd + 1) * 128], ropeb_ref)
        qb_ref[0, :, hd * 128:(hd + 1) * 128] = (t * scale_b).astype(BF16)
    ckv = _rms(h[:, _OFF_BCKV:_OFF_BCKV + B_KV_RANK], bkvn_ref[...]).astype(BF16)
    kv = _dot(ckv, wkv_ref[...])
    kr = _rope(h[:, _OFF_BKR:_OFF_BKR + 128], ropeb_ref)
    for hd in range(B_HEADS):
        ktb_ref[0, hd, 0] = (kv[:, hd * 128:(hd + 1) * 128] + kr).T.astype(BF16)
    for grp in range(2):
        v0, v1 = _with_ones(kv[:, 512 + grp * 128:512 + (grp + 1) * 128])
        vb_ref[0, :, (2 * grp) * 128:(2 * grp + 1) * 128] = v0.astype(BF16)
        vb_ref[0, :, (2 * grp + 1) * 128:(2 * grp + 2) * 128] = v1.astype(BF16)

    h = project(_SEC_C)
    cqh = h[:, _OFF_CQ:_OFF_CQ + 256]
    cqh = cqh * lax.rsqrt(_head_mean_sq(cqh) + EPS) * cqn_ref[...]
    for grp in range(2):
        t = _rope(cqh[:, grp * 128:(grp + 1) * 128], ropec_ref)
        qc_ref[0, :, grp * 128:(grp + 1) * 128] = (t * (HEAD_DIM ** -0.5 * LOG2E)).astype(BF16)
    ckh = h[:, _OFF_CK:_OFF_CK + 128]
    ckh = ckh * lax.rsqrt(_head_mean_sq(ckh) + EPS) * ckn_ref[...]
    ckh = _rope(ckh, ropec_ref)
    k0, k1 = _dup_halves(ckh)
    ktc_ref[0, 0, 0] = k0.T.astype(BF16)
    ktc_ref[0, 1, 0] = k1.T.astype(BF16)
    v0, v1 = _with_ones(h[:, _OFF_CV:_OFF_CV + 128])
    vc_ref[0, 0] = v0.astype(BF16)
    vc_ref[0, 1] = v1.astype(BF16)

    h = project(_SEC_D)
    lane = lax.broadcasted_iota(jnp.int32, (TM, LANES), 1)
    for grp in range(2):
        qg = h[:, _OFF_DQ + grp * 128:_OFF_DQ + (grp + 1) * 128] * (D_QK ** -0.5 * LOG2E)
        for u in range(4):
            t = qg if u == 0 else pltpu.roll(qg, LANES - D_QK * u, 1)
            qd_ref[0, :, (4 * grp + u) * LANES:(4 * grp + u + 1) * LANES] = (
                jnp.where(lane < D_QK, t, 0.0).astype(BF16))
    for grp in range(2):
        ktd_ref[0, grp, 0] = h[:, _OFF_DK + grp * 128:_OFF_DK + (grp + 1) * 128].T.astype(BF16)
    for grp in range(2):
        v0, v1 = _with_ones(h[:, _OFF_DV + grp * 128:_OFF_DV + (grp + 1) * 128])
        vd_ref[0, :, (2 * grp) * 128:(2 * grp + 1) * 128] = v0.astype(BF16)
        vd_ref[0, :, (2 * grp + 1) * 128:(2 * grp + 2) * 128] = v1.astype(BF16)


def _const_spec(shape):
    nd = len(shape)
    return pl.BlockSpec(shape, lambda *_: (0,) * nd)


def _per_batch_spec(a):
    return pl.BlockSpec((1,) + a.shape[1:], lambda b, i: (b,) + (0,) * (a.ndim - 1))


def _proj_call(x, lw, rope_b, rope_c):
    bsz, s_len, _ = x.shape
    nt = s_len // TM
    tok = lambda w: pl.BlockSpec((1, TM, w), lambda b, i: (b, i, 0))
    dup = pl.BlockSpec((1, 2, TM, 128), lambda b, i: (b, 0, i, 0))
    kt = lambda n: pl.BlockSpec((1, n, 1, 128, TK), lambda b, i: (b, 0, i, 0, 0))
    sds = jax.ShapeDtypeStruct
    out_shape = (
        sds((bsz, s_len, 256), BF16), sds((bsz, 2, s_len // BLOCK, 128, BLOCK), BF16),
        sds((bsz, 2, s_len, 128), BF16),
        sds((bsz, s_len, 512), BF16), sds((bsz, 4, nt, 128, TK), BF16), sds((bsz, s_len, 512), BF16),
        sds((bsz, s_len, 256), BF16), sds((bsz, 2, nt, 128, TK), BF16), sds((bsz, 2, s_len, 128), BF16),
        sds((bsz, s_len, 1024), BF16), sds((bsz, 2, nt, 128, TK), BF16), sds((bsz, s_len, 512), BF16),
    )
    kta = pl.BlockSpec((1, 2, TM // BLOCK, 128, BLOCK), lambda b, i: (b, 0, i, 0, 0))
    out_specs = (tok(256), kta, dup, tok(512), kt(4), tok(512),
                 tok(256), kt(2), dup, tok(1024), kt(2), tok(512))
    rope_spec = pl.BlockSpec((3, TM, 128), lambda b, i: (0, i, 0))
    in_specs = [
        tok(D_MODEL), _const_spec((1, D_MODEL)), _const_spec((D_MODEL, IN_PACKED)),
        _const_spec((1, B_Q_RANK)), _const_spec((B_Q_RANK, 512)),
        _const_spec((1, B_KV_RANK)), _const_spec((B_KV_RANK, 768)),
        _const_spec((1, 256)), _const_spec((1, 128)),
        rope_spec, rope_spec,
    ]
    return pl.pallas_call(
        _proj_kernel,
        grid=(bsz, nt),
        in_specs=in_specs,
        out_specs=out_specs,
        out_shape=out_shape,
        compiler_params=pltpu.CompilerParams(
            dimension_semantics=("arbitrary", "arbitrary"), vmem_limit_bytes=VMEM_LIMIT),
        name="in_proj",
    )(x, lw["g_attn"], lw["w_in"], lw["b_q_norm"], lw["w_q_up"], lw["b_kv_norm"], lw["w_kv_up"],
      lw["c_q_norm"], lw["c_k_norm"], rope_b, rope_c)


def _fold(op, s):
    out = s[:, 0:LANES]
    for j in range(1, s.shape[1] // LANES):
        out = op(out, s[:, j * LANES:(j + 1) * LANES])
    return out


def _attend(streams, n_chunks, s_scr, add_at=None, m_floor=None):
    n = len(streams)
    rows = s_scr.shape[2]
    m_run = [jnp.full((rows, LANES), -jnp.inf, F32)] * n
    acc = [jnp.zeros((rows, LANES), F32)] * n
    m = [None] * n
    for k in range(n + 1):
        for c in range(n_chunks):
            if k < n:
                qm, kt_at, _ = streams[k]
                s = _dot(qm(c) if callable(qm) else qm, kt_at(c))
                bias = None if add_at is None else add_at(k, c)
                if bias is not None:
                    s = s + bias
                s_scr[k % 2, c] = s
                m_run[k] = jnp.maximum(m_run[k], _fold(jnp.maximum, s))
            if k >= 1:
                j = k - 1
                p = jnp.exp2((s_scr[j % 2, c] - m[j]).astype(BF16))
                acc[j] = acc[j] + _dot(p, streams[j][2](c))
        if k < n:
            m[k] = jnp.max(m_run[k], axis=-1, keepdims=True)
            if m_floor is not None:
                m[k] = jnp.maximum(m[k], m_floor[k])
    return acc, m


def _normalised(acc):
    return acc / acc[:, 64:65]


def _pair_lanes(o0, o1):
    lane = lax.broadcasted_iota(jnp.int32, o0.shape, 1)
    return jnp.where(lane < 64, o0, pltpu.roll(o1, 64, 1))


def _half_masks(q):
    lane = lax.broadcasted_iota(jnp.int32, q.shape, 1)
    zero = jnp.zeros_like(q)
    return jnp.where(lane < 64, q, zero), jnp.where(lane < 64, zero, q)


def _attn_b_kernel(q_ref, kt_ref, v_ref, o_ref, s_scr, *, n_chunks):
    streams = [(q_ref[0, :, h * 128:(h + 1) * 128], lambda c, h=h: kt_ref[0, h, c],
                lambda c, h=h: v_ref[0, c, :, h * 128:(h + 1) * 128]) for h in range(B_HEADS)]
    acc, _ = _attend(streams, n_chunks, s_scr)
    for g in range(2):
        o_ref[0, :, g * 128:(g + 1) * 128] = _pair_lanes(
            _normalised(acc[2 * g]), _normalised(acc[2 * g + 1])).astype(BF16)


def _attn_c_kernel(q_ref, kt_ref, v_ref, o_ref, s_scr, *, n_chunks):
    streams = []
    for g in range(2):
        for qm in _half_masks(q_ref[0, :, g * 128:(g + 1) * 128]):
            streams.append((qm, lambda c, g=g: kt_ref[0, g, c], lambda c, g=g: v_ref[0, g, c]))
    acc, _ = _attend(streams, n_chunks, s_scr)
    for g in range(2):
        o_ref[0, :, g * 128:(g + 1) * 128] = _pair_lanes(
            _normalised(acc[2 * g]), _normalised(acc[2 * g + 1])).astype(BF16)


def _attn_d_kernel(q_ref, kt_ref, v_ref, lq1_ref, lk1_ref, lq2_ref, lk2_ref, sn_ref, o_ref, s_scr,
                   bias_scr, *, n_chunks, lam_init):
    tq = q_ref.shape[1]
    n_heads = q_ref.shape[2] // (2 * LANES)
    q0 = pl.program_id(1) * tq
    diag = q0 // TK
    lam = (jnp.exp(jnp.sum(lq1_ref[...] * lk1_ref[...], axis=-1, keepdims=True))
           - jnp.exp(jnp.sum(lq2_ref[...] * lk2_ref[...], axis=-1, keepdims=True)) + lam_init)
    lane = lax.broadcasted_iota(jnp.int32, (tq, LANES), 1)
    row = lax.broadcasted_iota(jnp.int32, (tq, LANES), 0)
    parts = D_SLOPE_PARTS

    f = lane - D_QK
    third = lambda idx, p: jnp.where(idx % 3 == 0, p[0], jnp.where(idx % 3 == 1, p[1], p[2]))
    q_feat, k_feat, sigma = [], [], []
    kr = lax.broadcasted_iota(jnp.int32, (LANES - D_QK, TK), 0)
    kc = lax.broadcasted_iota(jnp.int32, (LANES - D_QK, TK), 1)
    row_lo = (row % 256).astype(F32)
    row_hi = (row - row % 256).astype(F32)
    for r in range(n_heads):
        left = jnp.where((f >= 0) & (f < 6), third(f, parts[r]),
                         jnp.where((f >= 6) & (f < 9), -row_lo, jnp.where((f >= 9) & (f < 12), -row_hi, 0.0)))
        q_feat.append((left.astype(BF16), (-left).astype(BF16)))
        sig_rows = third(kr, parts[r])
        k_feat.append(jnp.where(kr < 3, (kc % 256).astype(F32),
                      jnp.where(kr < 6, (kc // 256 * 256).astype(F32),
                      jnp.where(kr < 12, sig_rows,
                      jnp.where(kr < 15, 256.0 * sig_rows, 0.0)))).astype(BF16))
        sigma.append(parts[r][0] + parts[r][1] + parts[r][2])
    off_lanes = (f >= 12) & (f < 15)

    def chunk_of(d):
        wrapped = diag + d >= n_chunks
        return jnp.where(wrapped, diag + d - n_chunks, diag + d), wrapped

    def q_at(u, d):
        qm = q_ref[0, :, u * LANES:(u + 1) * LANES]
        if d == 0:
            return qm
        c, is_left = chunk_of(d)
        off = ((q0 - c * TK) // 256).astype(F32)
        feat = jnp.where(off_lanes, jnp.where(is_left, -off, off).astype(BF16),
                         jnp.where(is_left, q_feat[u // 2][0], q_feat[u // 2][1]))
        return jnp.where(lane < D_QK, qm, feat)

    def kt_at(u, d):
        c, _ = chunk_of(d)
        return jnp.concatenate(
            [kt_ref[0, u // 4, c, (u % 4) * D_QK:(u % 4 + 1) * D_QK, :], k_feat[u // 2]], axis=0)

    def v_at(u, d):
        c, _ = chunk_of(d)
        return v_ref[0, c, :, (u // 2) * LANES:(u // 2 + 1) * LANES]

    def add_at(k, d):
        if d != 0:
            return None
        if k % 2 == 1:
            return bias_scr[0]
        rel = (lax.broadcasted_iota(jnp.int32, (tq, TK), 0)
               - lax.broadcasted_iota(jnp.int32, (tq, TK), 1)).astype(F32)
        bias = -sigma[k // 2] * jnp.abs(rel + (q0 - diag * TK).astype(F32))
        bias_scr[0] = bias
        return bias

    streams = [(functools.partial(q_at, u), functools.partial(kt_at, u), functools.partial(v_at, u))
               for u in range(2 * n_heads)]
    acc, _ = _attend(streams, n_chunks, s_scr, add_at)
    outs = []
    for r in range(n_heads):
        o = _normalised(acc[2 * r]) - lam * _normalised(acc[2 * r + 1])
        msq = jnp.sum(jnp.where(lane < 64, o * o, 0.0), axis=-1, keepdims=True) * (1.0 / D_V)
        outs.append(o * lax.rsqrt(msq + EPS) * sn_ref[...] * (1.0 - lam_init))
    for g in range(n_heads // 2):
        o_ref[0, :, g * LANES:(g + 1) * LANES] = _pair_lanes(outs[2 * g], outs[2 * g + 1]).astype(BF16)


def _full_attn_call(kind, q, kt, v, extra=(), lam_init=0.0):
    bsz, s_len = q.shape[0], q.shape[1]
    tq = TQ_D if kind == "d" else TQ
    nq, nc = s_len // tq, s_len // TK
    tok = lambda w: pl.BlockSpec((1, tq, w), lambda b, i: (b, i, 0))
    scratch = [pltpu.VMEM((2, nc, tq, TK), F32)]
    if kind == "b":
        body = functools.partial(_attn_b_kernel, n_chunks=nc)
        in_specs = [tok(512), _per_batch_spec(kt), _per_batch_spec(v)]
    elif kind == "c":
        body = functools.partial(_attn_c_kernel, n_chunks=nc)
        in_specs = [tok(256), _per_batch_spec(kt), _per_batch_spec(v)]
    else:
        body = functools.partial(_attn_d_kernel, n_chunks=nc, lam_init=lam_init)
        in_specs = [tok(1024), _per_batch_spec(kt), _per_batch_spec(v),
                    _const_spec((1, D_QK)), _const_spec((1, D_QK)),
                    _const_spec((1, D_QK)), _const_spec((1, D_QK)),
                    _const_spec((1, 128))]
        scratch.append(pltpu.VMEM((1, tq, TK), F32))
    return pl.pallas_call(
        body,
        grid=(bsz, nq),
        in_specs=in_specs,
        out_specs=tok(256),
        out_shape=jax.ShapeDtypeStruct((bsz, s_len, 256), BF16),
        scratch_shapes=scratch,
        compiler_params=pltpu.CompilerParams(
            dimension_semantics=("arbitrary", "arbitrary"), vmem_limit_bytes=VMEM_LIMIT),
        name="attn_" + kind,
    )(q, kt, v, *extra)


def _attn_a_kernel(q_ref, kt_ref, v_ref, sink_ref, o_ref, s_scr, *, s_len):
    n_blocks = s_len // BLOCK
    per_chunk = TKA // BLOCK
    sinks = [sink_ref[:, h:h + 1] * LOG2E for h in range(A_HEADS)]
    streams, neg_dist = [], []
    for t in range(SUB_A):
        q0 = (pl.program_id(1) * SUB_A + t) * TQA
        first = q0 // BLOCK - 1

        def key_blocks(c, first=first):
            return [jnp.clip(first + per_chunk * c + j, 0, n_blocks - 1) for j in range(per_chunk)]

        def kt_at(g, c, key_blocks=key_blocks):
            return jnp.concatenate([kt_ref[0, g, i] for i in key_blocks(c)], axis=1)

        def v_at(g, c, key_blocks=key_blocks):
            return jnp.concatenate(
                [v_ref[0, g, pl.ds(pl.multiple_of(i * BLOCK, BLOCK), BLOCK), :] for i in key_blocks(c)],
                axis=0)

        per_c = []
        for c in range(N_CHUNKS_A):
            col = lax.broadcasted_iota(jnp.int32, (TQA, TKA), 1)
            diff = lax.broadcasted_iota(jnp.int32, (TQA, TKA), 0) - col + (BLOCK - TKA * c)
            kpos = q0 - BLOCK + TKA * c + col
            valid = (jnp.abs(diff) <= WINDOW) & (kpos >= 0) & (kpos < s_len)
            per_c.append(jnp.where(valid, -jnp.abs(diff).astype(F32), -jnp.inf))
        neg_dist.append(per_c)
        for g in range(2):
            for qm in _half_masks(q_ref[0, t * TQA:(t + 1) * TQA, g * 128:(g + 1) * 128]):
                streams.append((qm, functools.partial(kt_at, g), functools.partial(v_at, g)))
    acc, m = _attend(streams, N_CHUNKS_A, s_scr,
                     add_at=lambda k, c: (SLOPES[k % A_HEADS] * LOG2E) * neg_dist[k // A_HEADS][c],
                     m_floor=sinks * SUB_A)
    for t in range(SUB_A):
        outs = []
        for h in range(A_HEADS):
            k = t * A_HEADS + h
            den = acc[k] + jnp.exp2(sinks[h] - m[k])
            outs.append(acc[k] / den[:, 64:65])
        for g in range(2):
            o_ref[0, t * TQA:(t + 1) * TQA, g * 128:(g + 1) * 128] = _pair_lanes(
                outs[2 * g], outs[2 * g + 1]).astype(BF16)


def _attn_a_call(q, kt, v, sink):
    bsz, s_len = q.shape[0], q.shape[1]
    tok = pl.BlockSpec((1, SUB_A * TQA, 256), lambda b, i: (b, i, 0))
    return pl.pallas_call(
        functools.partial(_attn_a_kernel, s_len=s_len),
        grid=(bsz, s_len // (SUB_A * TQA)),
        in_specs=[tok, _per_batch_spec(kt), _per_batch_spec(v), _const_spec((1, A_HEADS))],
        out_specs=tok,
        out_shape=jax.ShapeDtypeStruct((bsz, s_len, 256), BF16),
        scratch_shapes=[pltpu.VMEM((2, N_CHUNKS_A, TQA, TKA), F32)],
        compiler_params=pltpu.CompilerParams(
            dimension_semantics=("arbitrary", "arbitrary"), vmem_limit_bytes=VMEM_LIMIT),
        name="attn_a",
    )(q, kt, v, sink)


def _out_proj_kernel(x_ref, oa_ref, ob_ref, oc_ref, od_ref, wo_ref, x1_ref):
    x1 = x_ref[0]
    for j, o_ref in enumerate((oa_ref, ob_ref, oc_ref, od_ref)):
        x1 = x1 + _dot(o_ref[0], wo_ref[j * 256:(j + 1) * 256, :])
    x1_ref[0] = x1


def _out_proj_call(x, oa, ob, oc, od, lw):
    bsz, s_len, _ = x.shape
    tok = lambda w: pl.BlockSpec((1, TM, w), lambda b, i: (b, i, 0))
    return pl.pallas_call(
        _out_proj_kernel,
        grid=(bsz, s_len // TM),
        in_specs=[tok(D_MODEL), tok(256), tok(256), tok(256), tok(256),
                  _const_spec((D_MODEL, D_MODEL))],
        out_specs=tok(D_MODEL),
        out_shape=jax.ShapeDtypeStruct((bsz, s_len, D_MODEL), F32),
        compiler_params=pltpu.CompilerParams(
            dimension_semantics=("arbitrary", "arbitrary"), vmem_limit_bytes=VMEM_LIMIT),
        name="out_proj",
    )(x, oa, ob, oc, od, lw["w_out"])


def _transpose8(blocks):
    sub = lax.broadcasted_iota(jnp.int32, blocks[0].shape, 0)
    for d in (4, 2, 1):
        keep = (sub & d) == 0
        nxt = list(blocks)
        for i in range(SUBLANES):
            if i & d == 0:
                lo, hi = blocks[i], blocks[i + d]
                nxt[i] = jnp.where(keep, lo, pltpu.roll(hi, d, 0))
                nxt[i + d] = jnp.where(keep, pltpu.roll(lo, SUBLANES - d, 0), hi)
        blocks = nxt
    return blocks


def _ffn_kernel(x1_ref, prev_ref, next_ref, g_ref, wu_ref, vec_ref, wd_ref, gf_ref,
                o_ref, xf_ref, xb_ref, h0_ref, h1_ref, act_ref, *, final_norm):
    first = pl.program_id(1) == 0
    last = pl.program_id(1) == pl.num_programs(1) - 1
    nv = TM // SUBLANES

    sub = lax.broadcasted_iota(jnp.int32, (SUBLANES, D_MODEL), 0)
    for jb in range(nv // SUBLANES):
        blocks = [x1_ref[0, s * nv + jb * SUBLANES:s * nv + (jb + 1) * SUBLANES, :] for s in range(SUBLANES)]
        for jj, rows in enumerate(_transpose8(blocks)):
            j = jb * SUBLANES + jj
            xf_ref[SUBLANES * (j + 1):SUBLANES * (j + 2), :] = rows
            if j == nv - 1:
                xf_ref[0:SUBLANES, :] = jnp.where(
                    sub == 0, prev_ref[0, SUBLANES - 1:SUBLANES, :], pltpu.roll(rows, 1, 0))
            if j == 0:
                xf_ref[TM + SUBLANES:TM + 2 * SUBLANES, :] = jnp.where(
                    sub == SUBLANES - 1, next_ref[0, 0:1, :], pltpu.roll(rows, SUBLANES - 1, 0))
    xb_ref[...] = _rms(xf_ref[...], g_ref[...]).astype(BF16)

    def up(c, h_ref):
        xe = xb_ref[...]
        h_ref[0] = _dot(xe, wu_ref[:, c * FF_CHUNK:(c + 1) * FF_CHUNK])
        h_ref[1] = _dot(xe, wu_ref[:, D_FF + c * FF_CHUNK:D_FF + (c + 1) * FF_CHUNK])

    sub_h = lax.broadcasted_iota(jnp.int32, (SUBLANES, FF_CHUNK), 0)

    def conv(h_ref, half, bias, w0, w1, w2, cb):
        top = h_ref[half, 0:SUBLANES, :]
        h_ref[half, 0:SUBLANES, :] = jnp.where(first & (sub_h == 0), -bias, top)
        bot = h_ref[half, TM + SUBLANES:TM + 2 * SUBLANES, :]
        h_ref[half, TM + SUBLANES:TM + 2 * SUBLANES, :] = jnp.where(last & (sub_h == SUBLANES - 1), -bias, bot)
        return (w0 * h_ref[half, 0:TM, :] + w1 * h_ref[half, SUBLANES:SUBLANES + TM, :]
                + w2 * h_ref[half, 2 * SUBLANES:2 * SUBLANES + TM, :] + (bias * (w0 + w1 + w2) + cb))

    def gate(c, h_ref):
        vec = vec_ref[c]
        ca = conv(h_ref, 0, vec[0:1], vec[2:3], vec[3:4], vec[4:5], vec[8:9])
        cb = conv(h_ref, 1, vec[1:2], vec[5:6], vec[6:7], vec[7:8], vec[9:10])
        act_ref[c] = ((ca / (1.0 + jnp.exp(-ca))) * cb).astype(BF16)

    bufs = (h0_ref, h1_ref)
    up(0, bufs[0])
    for c in range(N_FF_CHUNKS):
        if c + 1 < N_FF_CHUNKS:
            up(c + 1, bufs[(c + 1) % 2])
        gate(c, bufs[c % 2])

    for n in range(D_MODEL // FF_CHUNK):
        cols = slice(n * FF_CHUNK, (n + 1) * FF_CHUNK)
        y = _dot(act_ref[0], wd_ref[0, :, cols])
        for c in range(1, N_FF_CHUNKS):
            y = y + _dot(act_ref[c], wd_ref[c, :, cols])
        xf_ref[0:TM, cols] = y
        for jb in range(nv // SUBLANES):
            groups = [xf_ref[(jb * SUBLANES + jj) * SUBLANES:(jb * SUBLANES + jj + 1) * SUBLANES, cols]
                      for jj in range(SUBLANES)]
            for s, blk in enumerate(_transpose8(groups)):
                rows = slice(s * nv + jb * SUBLANES, s * nv + (jb + 1) * SUBLANES)
                o_ref[0, rows, cols] = x1_ref[0, rows, cols] + blk
    if final_norm:
        o_ref[0] = _rms(o_ref[0], gf_ref[...])


def _ffn_call(x1, lw, g_final, final_norm):
    bsz, s_len, _ = x1.shape
    per_tile = TM // SUBLANES
    last = s_len // SUBLANES - 1
    tok = pl.BlockSpec((1, TM, D_MODEL), lambda b, i: (b, i, 0))
    prev = pl.BlockSpec((1, SUBLANES, D_MODEL), lambda b, i: (b, jnp.maximum(i * per_tile - 1, 0), 0))
    nxt = pl.BlockSpec((1, SUBLANES, D_MODEL),
                       lambda b, i: (b, jnp.minimum((i + 1) * per_tile, last), 0))
    rows = TM + 2 * SUBLANES
    return pl.pallas_call(
        functools.partial(_ffn_kernel, final_norm=final_norm),
        grid=(bsz, s_len // TM),
        in_specs=[tok, prev, nxt, _const_spec((1, D_MODEL)),
                  _const_spec((D_MODEL, 2 * D_FF)),
                  _const_spec((N_FF_CHUNKS, 16, FF_CHUNK)),
                  _const_spec((N_FF_CHUNKS, FF_CHUNK, D_MODEL)),
                  _const_spec((1, D_MODEL))],
        out_specs=tok,
        out_shape=jax.ShapeDtypeStruct((bsz, s_len, D_MODEL), F32),
        scratch_shapes=[pltpu.VMEM((rows, D_MODEL), F32),
                        pltpu.VMEM((rows, D_MODEL), BF16),
                        pltpu.VMEM((2, rows, FF_CHUNK), F32),
                        pltpu.VMEM((2, rows, FF_CHUNK), F32),
                        pltpu.VMEM((N_FF_CHUNKS, TM, FF_CHUNK), BF16)],
        compiler_params=pltpu.CompilerParams(
            dimension_semantics=("arbitrary", "arbitrary"), vmem_limit_bytes=VMEM_LIMIT),
        name="ffn",
    )(x1, x1, x1, lw["g_ffn"], lw["w_up"], lw["ffn_vec"], lw["w_down"], g_final)


def _pack_layer(p, l):
    pieces = jnp.split(p["w_in"][l], np.cumsum(IN_SIZES)[:-1].tolist(), axis=1)
    aq, ak, av, bcq, bckv, bkr, cq, ck, cv, dq, dk, dv = pieces
    z = lambda n: jnp.zeros((D_MODEL, n), F32)
    w_in = jnp.concatenate([aq, ak, av, bcq, bckv, z(64), bkr, z(32), cq, ck, cv, dq, dk, dv], axis=1)

    wq = p["b_w_q_up"][l].reshape(B_Q_RANK, B_HEADS, B_NOPE + B_ROPE)
    wq = jnp.pad(wq, ((0, 0), (0, 0), (0, 128 - B_NOPE - B_ROPE))).reshape(B_Q_RANK, B_HEADS * 128)
    wkv = p["b_w_kv_up"][l].reshape(B_KV_RANK, B_HEADS, B_NOPE + B_V)
    wk = jnp.pad(wkv[:, :, :B_NOPE], ((0, 0), (0, 0), (0, 128 - B_NOPE))).reshape(B_KV_RANK, B_HEADS * 128)
    wv = wkv[:, :, B_NOPE:].reshape(B_KV_RANK, B_HEADS * B_V)

    halves = lambda v: (v[..., :D_FF], v[..., D_FF:])
    bua, bub = halves(p["b_up"][l])
    cwa, cwb = halves(p["conv_w"][l])
    cba, cbb = halves(p["conv_b"][l])
    rows = [bua, bub, cwa[0], cwa[1], cwa[2], cwb[0], cwb[1], cwb[2], cba, cbb]
    vec = jnp.stack(rows + [jnp.zeros_like(bua)] * (16 - len(rows)), axis=0)
    vec = vec.reshape(16, N_FF_CHUNKS, FF_CHUNK).transpose(1, 0, 2)
    row = lambda v: v.reshape(1, -1)
    return {
        "g_attn": row(p["g_attn"][l]),
        "w_in": w_in.astype(BF16),
        "a_sink": row(p["a_sink"][l]),
        "b_q_norm": row(p["b_q_norm"][l]),
        "w_q_up": wq.astype(BF16),
        "b_kv_norm": row(p["b_kv_norm"][l]),
        "w_kv_up": jnp.concatenate([wk, wv], axis=1).astype(BF16),
        "c_q_norm": row(jnp.tile(p["c_q_norm"][l], 4)),
        "c_k_norm": row(jnp.tile(p["c_k_norm"][l], 2)),
        "d_lambda": tuple(row(p[n][l]) for n in ("d_lambda_q1", "d_lambda_k1", "d_lambda_q2", "d_lambda_k2")),
        "d_sub_norm": row(jnp.tile(p["d_sub_norm"][l], 2)),
        "w_out": p["w_out"][l].astype(BF16),
        "g_ffn": row(p["g_ffn"][l]),
        "w_up": p["w_up"][l].astype(BF16),
        "ffn_vec": vec,
        "w_down": p["w_down"][l].reshape(N_FF_CHUNKS, FF_CHUNK, D_MODEL).astype(BF16),
    }


def _rope_tables(s_len):
    half = 16
    inv = ROPE_THETA ** (-jnp.arange(half, dtype=F32) * 2.0 / (2 * half))
    pos = jnp.arange(s_len, dtype=F32)
    rows = s_len // GRID_W
    row_pos = jnp.broadcast_to(jnp.arange(rows, dtype=F32)[:, None], (rows, GRID_W)).reshape(s_len)
    col_pos = jnp.broadcast_to(jnp.arange(GRID_W, dtype=F32)[None, :], (rows, GRID_W)).reshape(s_len)

    def cs(p):
        ang = p[:, None] * inv[None, :]
        return jnp.cos(ang), jnp.sin(ang)

    one = lambda n: jnp.ones((s_len, n), F32)
    zero = lambda n: jnp.zeros((s_len, n), F32)
    c, s = cs(pos)
    rope_b = jnp.stack([
        jnp.concatenate([one(64), c, c, one(32)], axis=1),
        jnp.concatenate([zero(64), -s, zero(16), zero(32)], axis=1),
        jnp.concatenate([zero(64), zero(16), s, zero(32)], axis=1)])
    cr, sr = cs(row_pos)
    cc, sc = cs(col_pos)
    z = zero(16)
    rope_c = jnp.stack([
        jnp.tile(jnp.concatenate([cr, cr, cc, cc], axis=1), (1, 2)),
        jnp.tile(jnp.concatenate([-sr, z, -sc, z], axis=1), (1, 2)),
        jnp.tile(jnp.concatenate([z, sr, z, sc], axis=1), (1, 2))])
    return rope_b, rope_c


def _trunk(x, layers, g_final):
    bsz, s_len, _ = x.shape
    nc = s_len // TK
    rope_b, rope_c = _rope_tables(s_len)
    depth = len(layers)
    for l, lw in enumerate(layers):
        qa, ka, va, qb, ktb, vb, qc, ktc, vc, qd, ktd, vd = _proj_call(x, lw, rope_b, rope_c)
        oa = _attn_a_call(qa, ka, va, lw["a_sink"])
        ob = _full_attn_call("b", qb, ktb, vb.reshape(bsz, nc, TK, 512))
        oc = _full_attn_call("c", qc, ktc, vc.reshape(bsz, 2, nc, TK, 128))
        lam_init = 0.8 - 0.6 * math.exp(-0.3 * l)
        od = _full_attn_call("d", qd, ktd, vd.reshape(bsz, nc, TK, 512),
                             extra=(*lw["d_lambda"], lw["d_sub_norm"]), lam_init=lam_init)
        x1 = _out_proj_call(x, oa, ob, oc, od, lw)
        x = _ffn_call(x1, lw, g_final, final_norm=(l == depth - 1))
    return x


def kernel(x_prompt, x_sample, g_attn, w_in, a_sink, b_q_norm, b_w_q_up, b_kv_norm, b_w_kv_up,
           c_q_norm, c_k_norm, d_lambda_q1, d_lambda_k1, d_lambda_q2, d_lambda_k2, d_sub_norm, w_out,
           g_ffn, w_up, b_up, conv_w, conv_b, w_down, g_final):
    p = dict(g_attn=g_attn, w_in=w_in, a_sink=a_sink, b_q_norm=b_q_norm, b_w_q_up=b_w_q_up,
             b_kv_norm=b_kv_norm, b_w_kv_up=b_w_kv_up, c_q_norm=c_q_norm, c_k_norm=c_k_norm,
             d_lambda_q1=d_lambda_q1, d_lambda_k1=d_lambda_k1, d_lambda_q2=d_lambda_q2,
             d_lambda_k2=d_lambda_k2, d_sub_norm=d_sub_norm, w_out=w_out, g_ffn=g_ffn, w_up=w_up,
             b_up=b_up, conv_w=conv_w, conv_b=conv_b, w_down=w_down)
    layers = [_pack_layer(p, l) for l in range(g_attn.shape[0])]
    gf = g_final.reshape(1, -1)
    return _trunk(x_prompt, layers, gf), _trunk(x_sample, layers, gf)
```

```python
import functools
import math

import jax
import jax.numpy as jnp
import numpy as np
from jax import lax
from jax.experimental import pallas as pl
from jax.experimental.pallas import tpu as pltpu

F32 = jnp.float32
BF16 = jnp.bfloat16

D_MODEL = 1024
GRID_W = 64
BLOCK = 128
HEAD_DIM = 64
EPS = 1e-6
ROPE_THETA = 10000.0

A_HEADS = 4
WINDOW = 128
B_HEADS = 4
B_Q_RANK = 256
B_KV_RANK = 128
B_NOPE = 64
B_ROPE = 32
B_V = 64
D_QK = 32
D_V = 64
N_ALIBI = 8
D_FF = 2816
IN_SIZES = (256, 128, 128, 256, 128, 32, 256, 128, 128, 256, 256, 256)

SLOPES = tuple(2.0 ** (-8.0 * (i + 1.0) / N_ALIBI) for i in range(N_ALIBI))
LOG2E = math.log2(math.e)


def _split3(x):
    rest, out = np.float32(x), []
    for _ in range(3):
        part = np.float32(np.asarray(rest).astype(jnp.bfloat16))
        out.append(float(part))
        rest = np.float32(rest - part)
    return tuple(out)


D_SLOPE_PARTS = tuple(_split3(SLOPES[A_HEADS + h] * LOG2E) for h in range(4))

LANES = 128
TM = 512
TK = 512
TQ = 512
TQ_D = 512
TQA = 256
SUB_A = 2
TKA = 512
N_CHUNKS_A = -(-(TQA + 2 * WINDOW) // TKA)
SUBLANES = 8
FF_CHUNK = 256
N_FF_CHUNKS = D_FF // FF_CHUNK
VMEM_LIMIT = 56 * 1024 * 1024

_SEC_A, _SEC_B, _SEC_C, _SEC_D = (0, 512), (512, 1024), (1024, 1536), (1536, 2304)
_OFF_AQ, _OFF_AK, _OFF_AV = 0, 256, 384
_OFF_BCQ, _OFF_BCKV, _OFF_BKR = 0, 256, 384
_OFF_CQ, _OFF_CK, _OFF_CV = 0, 256, 384
_OFF_DQ, _OFF_DK, _OFF_DV = 0, 256, 512
IN_PACKED = 2304


def _dot(a, b):
    return jnp.dot(a, b, preferred_element_type=F32)


def _rms(x, g):
    return x * lax.rsqrt(jnp.mean(x * x, axis=-1, keepdims=True) + EPS) * g


def _rope(t, tab_ref):
    return (t * tab_ref[0] + pltpu.roll(t, LANES - 16, 1) * tab_ref[1]
            + pltpu.roll(t, 16, 1) * tab_ref[2])


def _dup_halves(t):
    lane = lax.broadcasted_iota(jnp.int32, t.shape, 1)
    sw = pltpu.roll(t, 64, 1)
    lo = lane < 64
    return jnp.where(lo, t, sw), jnp.where(lo, sw, t)


def _with_ones(t):
    lane = lax.broadcasted_iota(jnp.int32, t.shape, 1)
    lo = lane < 64
    return jnp.where(lo, t, 1.0), jnp.where(lo, pltpu.roll(t, 64, 1), 1.0)


def _head_mean_sq(t):
    w = t.shape[1]
    r = lax.broadcasted_iota(jnp.int32, (w, w), 0) // HEAD_DIM
    c = lax.broadcasted_iota(jnp.int32, (w, w), 1) // HEAD_DIM
    pool = jnp.where(r == c, 1.0 / HEAD_DIM, 0.0).astype(BF16)
    t2 = t * t
    hi = t2.astype(BF16)
    lo = (t2 - hi.astype(F32)).astype(BF16)
    return _dot(hi, pool) + _dot(lo, pool)


def _proj_kernel(x_ref, g_ref, win_ref, *refs):
    *post_refs, ha_scr, hb_scr = refs
    step = pl.program_id(0)

    def run(new_scr, prev_scr):
        new_scr[...] = _dot(_rms(x_ref[0], g_ref[...]).astype(BF16), win_ref[...])
        _proj_post(lambda sec: prev_scr[:, sec[0]:sec[1]], *post_refs)

    @pl.when(step == 0)
    def _():
        run(ha_scr, ha_scr)

    @pl.when(step % 2 == 1)
    def _():
        run(hb_scr, ha_scr)

    @pl.when((step > 0) & (step % 2 == 0))
    def _():
        run(ha_scr, hb_scr)


def _proj_post(project, bqn_ref, wq_ref, bkvn_ref, wkv_ref, cqn_ref, ckn_ref, ropeb_ref, ropec_ref,
               qa_ref, ka_ref, va_ref, qb_ref, ktb_ref, vb_ref, qc_ref, ktc_ref, vc_ref,
               qd_ref, ktd_ref, vd_ref):

    h = project(_SEC_A)
    qa_ref[0] = (h[:, _OFF_AQ:_OFF_AQ + 256] * (HEAD_DIM ** -0.5 * LOG2E)).astype(BF16)
    for grp, kd in enumerate(_dup_halves(h[:, _OFF_AK:_OFF_AK + 128])):
        for j in range(TM // BLOCK):
            ka_ref[0, grp, j] = kd[j * BLOCK:(j + 1) * BLOCK].T.astype(BF16)
    v0, v1 = _with_ones(h[:, _OFF_AV:_OFF_AV + 128])
    va_ref[0, 0] = v0.astype(BF16)
    va_ref[0, 1] = v1.astype(BF16)

    h = project(_SEC_B)
    cq = _rms(h[:, _OFF_BCQ:_OFF_BCQ + B_Q_RANK], bqn_ref[...]).astype(BF16)
    qb = _dot(cq, wq_ref[...])
    scale_b = (B_NOPE + B_ROPE) ** -0.5 * LOG2E
    for hd in range(B_HEADS):
        t = _rope(qb[:, hd * 128:(hd + 1) * 128], ropeb_ref)
        qb_ref[0, :, hd * 128:(hd + 1) * 128] = (t * scale_b).astype(BF16)
    ckv = _rms(h[:, _OFF_BCKV:_OFF_BCKV + B_KV_RANK], bkvn_ref[...]).astype(BF16)
    kv = _dot(ckv, wkv_ref[...])
    kr = _rope(h[:, _OFF_BKR:_OFF_BKR + 128], ropeb_ref)
    for hd in range(B_HEADS):
        ktb_ref[0, hd, 0] = (kv[:, hd * 128:(hd + 1) * 128] + kr).T.astype(BF16)
    for grp in range(2):
        v0, v1 = _with_ones(kv[:, 512 + grp * 128:512 + (grp + 1) * 128])
        vb_ref[0, :, (2 * grp) * 128:(2 * grp + 1) * 128] = v0.astype(BF16)
        vb_ref[0, :, (2 * grp + 1) * 128:(2 * grp + 2) * 128] = v1.astype(BF16)

    h = project(_SEC_C)
    cqh = h[:, _OFF_CQ:_OFF_CQ + 256]
    cqh = cqh * lax.rsqrt(_head_mean_sq(cqh) + EPS) * cqn_ref[...]
    for grp in range(2):
        t = _rope(cqh[:, grp * 128:(grp + 1) * 128], ropec_ref)
        qc_ref[0, :, grp * 128:(grp + 1) * 128] = (t * (HEAD_DIM ** -0.5 * LOG2E)).astype(BF16)
    ckh = h[:, _OFF_CK:_OFF_CK + 128]
    ckh = ckh * lax.rsqrt(_head_mean_sq(ckh) + EPS) * ckn_ref[...]
    ckh = _rope(ckh, ropec_ref)
    k0, k1 = _dup_halves(ckh)
    ktc_ref[0, 0, 0] = k0.T.astype(BF16)
    ktc_ref[0, 1, 0] = k1.T.astype(BF16)
    v0, v1 = _with_ones(h[:, _OFF_CV:_OFF_CV + 128])
    vc_ref[0, 0] = v0.astype(BF16)
    vc_ref[0, 1] = v1.astype(BF16)

    h = project(_SEC_D)
    lane = lax.broadcasted_iota(jnp.int32, (TM, LANES), 1)
    for grp in range(2):
        qg = h[:, _OFF_DQ + grp * 128:_OFF_DQ + (grp + 1) * 128] * (D_QK ** -0.5 * LOG2E)
        for u in range(4):
            t = qg if u == 0 else pltpu.roll(qg, LANES - D_QK * u, 1)
            qd_ref[0, :, (4 * grp + u) * LANES:(4 * grp + u + 1) * LANES] = (
                jnp.where(lane < D_QK, t, 0.0).astype(BF16))
    for grp in range(2):
        ktd_ref[0, grp, 0] = h[:, _OFF_DK + grp * 128:_OFF_DK + (grp + 1) * 128].T.astype(BF16)
    for grp in range(2):
        v0, v1 = _with_ones(h[:, _OFF_DV + grp * 128:_OFF_DV + (grp + 1) * 128])
        vd_ref[0, :, (2 * grp) * 128:(2 * grp + 1) * 128] = v0.astype(BF16)
        vd_ref[0, :, (2 * grp + 1) * 128:(2 * grp + 2) * 128] = v1.astype(BF16)


def _const_spec(shape):
    nd = len(shape)
    return pl.BlockSpec(shape, lambda *_: (0,) * nd)


def _per_batch_spec(a):
    return pl.BlockSpec((1,) + a.shape[1:], lambda b, i: (b,) + (0,) * (a.ndim - 1))


def _proj_call(x, lw, rope_b, rope_c):
    bsz, s_len, _ = x.shape
    nt = s_len // TM
    n_tiles = bsz * nt
    src = lambda i: jnp.minimum(i, n_tiles - 1)
    dst = lambda i: jnp.maximum(i - 1, 0)
    tok = lambda w: pl.BlockSpec((1, TM, w), lambda i: (dst(i) // nt, dst(i) % nt, 0))
    dup = pl.BlockSpec((1, 2, TM, 128), lambda i: (dst(i) // nt, 0, dst(i) % nt, 0))
    kt = lambda n: pl.BlockSpec((1, n, 1, 128, TK), lambda i: (dst(i) // nt, 0, dst(i) % nt, 0, 0))
    sds = jax.ShapeDtypeStruct
    out_shape = (
        sds((bsz, s_len, 256), BF16), sds((bsz, 2, s_len // BLOCK, 128, BLOCK), BF16),
        sds((bsz, 2, s_len, 128), BF16),
        sds((bsz, s_len, 512), BF16), sds((bsz, 4, nt, 128, TK), BF16), sds((bsz, s_len, 512), BF16),
        sds((bsz, s_len, 256), BF16), sds((bsz, 2, nt, 128, TK), BF16), sds((bsz, 2, s_len, 128), BF16),
        sds((bsz, s_len, 1024), BF16), sds((bsz, 2, nt, 128, TK), BF16), sds((bsz, s_len, 512), BF16),
    )
    kta = pl.BlockSpec((1, 2, TM // BLOCK, 128, BLOCK), lambda i: (dst(i) // nt, 0, dst(i) % nt, 0, 0))
    out_specs = (tok(256), kta, dup, tok(512), kt(4), tok(512),
                 tok(256), kt(2), dup, tok(1024), kt(2), tok(512))
    rope_spec = pl.BlockSpec((3, TM, 128), lambda i: (0, dst(i) % nt, 0))
    in_specs = [
        pl.BlockSpec((1, TM, D_MODEL), lambda i: (src(i) // nt, src(i) % nt, 0)),
        _const_spec((1, D_MODEL)), _const_spec((D_MODEL, IN_PACKED)),
        _const_spec((1, B_Q_RANK)), _const_spec((B_Q_RANK, 512)),
        _const_spec((1, B_KV_RANK)), _const_spec((B_KV_RANK, 768)),
        _const_spec((1, 256)), _const_spec((1, 128)),
        rope_spec, rope_spec,
    ]
    return pl.pallas_call(
        _proj_kernel,
        grid=(n_tiles + 1,),
        in_specs=in_specs,
        out_specs=out_specs,
        out_shape=out_shape,
        scratch_shapes=[pltpu.VMEM((TM, IN_PACKED), F32), pltpu.VMEM((TM, IN_PACKED), F32)],
        compiler_params=pltpu.CompilerParams(
            dimension_semantics=("arbitrary",), vmem_limit_bytes=VMEM_LIMIT),
        name="in_proj",
    )(x, lw["g_attn"], lw["w_in"], lw["b_q_norm"], lw["w_q_up"], lw["b_kv_norm"], lw["w_kv_up"],
      lw["c_q_norm"], lw["c_k_norm"], rope_b, rope_c)


def _fold(op, s):
    out = s[:, 0:LANES]
    for j in range(1, s.shape[1] // LANES):
        out = op(out, s[:, j * LANES:(j + 1) * LANES])
    return out


def _attend(streams, n_chunks, s_scr, add_at=None, m_floor=None):
    n = len(streams)
    rows = s_scr.shape[2]
    m_run = [jnp.full((rows, LANES), -jnp.inf, F32)] * n
    acc = [jnp.zeros((rows, LANES), F32)] * n
    m = [None] * n
    for k in range(n + 1):
        for c in range(n_chunks):
            if k < n:
                qm, kt_at, _ = streams[k]
                s = _dot(qm(c) if callable(qm) else qm, kt_at(c))
                bias = None if add_at is None else add_at(k, c)
                if bias is not None:
                    s = s + bias
                s_scr[k % 2, c] = s
                m_run[k] = jnp.maximum(m_run[k], _fold(jnp.maximum, s))
            if k >= 1:
                j = k - 1
                p = jnp.exp2((s_scr[j % 2, c] - m[j]).astype(BF16))
                acc[j] = acc[j] + _dot(p, streams[j][2](c))
        if k < n:
            m[k] = jnp.max(m_run[k], axis=-1, keepdims=True)
            if m_floor is not None:
                m[k] = jnp.maximum(m[k], m_floor[k])
    return acc, m


def _normalised(acc):
    return acc / acc[:, 64:65]


def _pair_lanes(o0, o1):
    lane = lax.broadcasted_iota(jnp.int32, o0.shape, 1)
    return jnp.where(lane < 64, o0, pltpu.roll(o1, 64, 1))


def _half_masks(q):
    lane = lax.broadcasted_iota(jnp.int32, q.shape, 1)
    zero = jnp.zeros_like(q)
    return jnp.where(lane < 64, q, zero), jnp.where(lane < 64, zero, q)


def _attn_b_kernel(q_ref, kt_ref, v_ref, o_ref, s_scr, *, n_chunks):
    streams = [(q_ref[0, :, h * 128:(h + 1) * 128], lambda c, h=h: kt_ref[0, h, c],
                lambda c, h=h: v_ref[0, c, :, h * 128:(h + 1) * 128]) for h in range(B_HEADS)]
    acc, _ = _attend(streams, n_chunks, s_scr)
    for g in range(2):
        o_ref[0, :, g * 128:(g + 1) * 128] = _pair_lanes(
            _normalised(acc[2 * g]), _normalised(acc[2 * g + 1])).astype(BF16)


def _attn_c_kernel(q_ref, kt_ref, v_ref, o_ref, s_scr, *, n_chunks):
    streams = []
    for g in range(2):
        for qm in _half_masks(q_ref[0, :, g * 128:(g + 1) * 128]):
            streams.append((qm, lambda c, g=g: kt_ref[0, g, c], lambda c, g=g: v_ref[0, g, c]))
    acc, _ = _attend(streams, n_chunks, s_scr)
    for g in range(2):
        o_ref[0, :, g * 128:(g + 1) * 128] = _pair_lanes(
            _normalised(acc[2 * g]), _normalised(acc[2 * g + 1])).astype(BF16)


def _attn_d_kernel(q_ref, kt_ref, v_ref, lq1_ref, lk1_ref, lq2_ref, lk2_ref, sn_ref, o_ref, s_scr,
                   bias_scr, *, n_chunks, lam_init):
    tq = q_ref.shape[1]
    n_heads = q_ref.shape[2] // (2 * LANES)
    q0 = pl.program_id(1) * tq
    diag = q0 // TK
    lam = (jnp.exp(jnp.sum(lq1_ref[...] * lk1_ref[...], axis=-1, keepdims=True))
           - jnp.exp(jnp.sum(lq2_ref[...] * lk2_ref[...], axis=-1, keepdims=True)) + lam_init)
    lane = lax.broadcasted_iota(jnp.int32, (tq, LANES), 1)
    row = lax.broadcasted_iota(jnp.int32, (tq, LANES), 0)
    parts = D_SLOPE_PARTS

    f = lane - D_QK
    third = lambda idx, p: jnp.where(idx % 3 == 0, p[0], jnp.where(idx % 3 == 1, p[1], p[2]))
    q_feat, k_feat, sigma = [], [], []
    kr = lax.broadcasted_iota(jnp.int32, (LANES - D_QK, TK), 0)
    kc = lax.broadcasted_iota(jnp.int32, (LANES - D_QK, TK), 1)
    row_lo = (row % 256).astype(F32)
    row_hi = (row - row % 256).astype(F32)
    for r in range(n_heads):
        left = jnp.where((f >= 0) & (f < 6), third(f, parts[r]),
                         jnp.where((f >= 6) & (f < 9), -row_lo, jnp.where((f >= 9) & (f < 12), -row_hi, 0.0)))
        q_feat.append((left.astype(BF16), (-left).astype(BF16)))
        sig_rows = third(kr, parts[r])
        k_feat.append(jnp.where(kr < 3, (kc % 256).astype(F32),
                      jnp.where(kr < 6, (kc // 256 * 256).astype(F32),
                      jnp.where(kr < 12, sig_rows,
                      jnp.where(kr < 15, 256.0 * sig_rows, 0.0)))).astype(BF16))
        sigma.append(parts[r][0] + parts[r][1] + parts[r][2])
    off_lanes = (f >= 12) & (f < 15)

    def chunk_of(d):
        wrapped = diag + d >= n_chunks
        return jnp.where(wrapped, diag + d - n_chunks, diag + d), wrapped

    def q_at(u, d):
        qm = q_ref[0, :, u * LANES:(u + 1) * LANES]
        if d == 0:
            return qm
        c, is_left = chunk_of(d)
        off = ((q0 - c * TK) // 256).astype(F32)
        feat = jnp.where(off_lanes, jnp.where(is_left, -off, off).astype(BF16),
                         jnp.where(is_left, q_feat[u // 2][0], q_feat[u // 2][1]))
        return jnp.where(lane < D_QK, qm, feat)

    def kt_at(u, d):
        c, _ = chunk_of(d)
        return jnp.concatenate(
            [kt_ref[0, u // 4, c, (u % 4) * D_QK:(u % 4 + 1) * D_QK, :], k_feat[u // 2]], axis=0)

    def v_at(u, d):
        c, _ = chunk_of(d)
        return v_ref[0, c, :, (u // 2) * LANES:(u // 2 + 1) * LANES]

    def add_at(k, d):
        if d != 0:
            return None
        if k % 2 == 1:
            return bias_scr[0]
        rel = (lax.broadcasted_iota(jnp.int32, (tq, TK), 0)
               - lax.broadcasted_iota(jnp.int32, (tq, TK), 1)).astype(F32)
        bias = -sigma[k // 2] * jnp.abs(rel + (q0 - diag * TK).astype(F32))
        bias_scr[0] = bias
        return bias

    streams = [(functools.partial(q_at, u), functools.partial(kt_at, u), functools.partial(v_at, u))
               for u in range(2 * n_heads)]
    acc, _ = _attend(streams, n_chunks, s_scr, add_at)
    outs = []
    for r in range(n_heads):
        o = _normalised(acc[2 * r]) - lam * _normalised(acc[2 * r + 1])
        msq = jnp.sum(jnp.where(lane < 64, o * o, 0.0), axis=-1, keepdims=True) * (1.0 / D_V)
        outs.append(o * lax.rsqrt(msq + EPS) * sn_ref[...] * (1.0 - lam_init))
    for g in range(n_heads // 2):
        o_ref[0, :, g * LANES:(g + 1) * LANES] = _pair_lanes(outs[2 * g], outs[2 * g + 1]).astype(BF16)


def _full_attn_call(kind, q, kt, v, extra=(), lam_init=0.0):
    bsz, s_len = q.shape[0], q.shape[1]
    tq = TQ_D if kind == "d" else TQ
    nq, nc = s_len // tq, s_len // TK
    tok = lambda w: pl.BlockSpec((1, tq, w), lambda b, i: (b, i, 0))
    scratch = [pltpu.VMEM((2, nc, tq, TK), F32)]
    if kind == "b":
        body = functools.partial(_attn_b_kernel, n_chunks=nc)
        in_specs = [tok(512), _per_batch_spec(kt), _per_batch_spec(v)]
    elif kind == "c":
        body = functools.partial(_attn_c_kernel, n_chunks=nc)
        in_specs = [tok(256), _per_batch_spec(kt), _per_batch_spec(v)]
    else:
        body = functools.partial(_attn_d_kernel, n_chunks=nc, lam_init=lam_init)
        in_specs = [tok(1024), _per_batch_spec(kt), _per_batch_spec(v),
                    _const_spec((1, D_QK)), _const_spec((1, D_QK)),
                    _const_spec((1, D_QK)), _const_spec((1, D_QK)),
                    _const_spec((1, 128))]
        scratch.append(pltpu.VMEM((1, tq, TK), F32))
    return pl.pallas_call(
        body,
        grid=(bsz, nq),
        in_specs=in_specs,
        out_specs=tok(256),
        out_shape=jax.ShapeDtypeStruct((bsz, s_len, 256), BF16),
        scratch_shapes=scratch,
        compiler_params=pltpu.CompilerParams(
            dimension_semantics=("arbitrary", "arbitrary"), vmem_limit_bytes=VMEM_LIMIT),
        name="attn_" + kind,
    )(q, kt, v, *extra)


def _attn_a_kernel(q_ref, kt_ref, v_ref, sink_ref, o_ref, s_scr, *, s_len):
    n_blocks = s_len // BLOCK
    per_chunk = TKA // BLOCK
    sinks = [sink_ref[:, h:h + 1] * LOG2E for h in range(A_HEADS)]
    streams, neg_dist = [], []
    for t in range(SUB_A):
        q0 = (pl.program_id(1) * SUB_A + t) * TQA
        first = q0 // BLOCK - 1

        def key_blocks(c, first=first):
            return [jnp.clip(first + per_chunk * c + j, 0, n_blocks - 1) for j in range(per_chunk)]

        def kt_at(g, c, key_blocks=key_blocks):
            return jnp.concatenate([kt_ref[0, g, i] for i in key_blocks(c)], axis=1)

        def v_at(g, c, key_blocks=key_blocks):
            return jnp.concatenate(
                [v_ref[0, g, pl.ds(pl.multiple_of(i * BLOCK, BLOCK), BLOCK), :] for i in key_blocks(c)],
                axis=0)

        per_c = []
        for c in range(N_CHUNKS_A):
            col = lax.broadcasted_iota(jnp.int32, (TQA, TKA), 1)
            diff = lax.broadcasted_iota(jnp.int32, (TQA, TKA), 0) - col + (BLOCK - TKA * c)
            kpos = q0 - BLOCK + TKA * c + col
            valid = (jnp.abs(diff) <= WINDOW) & (kpos >= 0) & (kpos < s_len)
            per_c.append(jnp.where(valid, -jnp.abs(diff).astype(F32), -jnp.inf))
        neg_dist.append(per_c)
        for g in range(2):
            for qm in _half_masks(q_ref[0, t * TQA:(t + 1) * TQA, g * 128:(g + 1) * 128]):
                streams.append((qm, functools.partial(kt_at, g), functools.partial(v_at, g)))
    acc, m = _attend(streams, N_CHUNKS_A, s_scr,
                     add_at=lambda k, c: (SLOPES[k % A_HEADS] * LOG2E) * neg_dist[k // A_HEADS][c],
                     m_floor=sinks * SUB_A)
    for t in range(SUB_A):
        outs = []
        for h in range(A_HEADS):
            k = t * A_HEADS + h
            den = acc[k] + jnp.exp2(sinks[h] - m[k])
            outs.append(acc[k] / den[:, 64:65])
        for g in range(2):
            o_ref[0, t * TQA:(t + 1) * TQA, g * 128:(g + 1) * 128] = _pair_lanes(
                outs[2 * g], outs[2 * g + 1]).astype(BF16)


def _attn_a_call(q, kt, v, sink):
    bsz, s_len = q.shape[0], q.shape[1]
    tok = pl.BlockSpec((1, SUB_A * TQA, 256), lambda b, i: (b, i, 0))
    return pl.pallas_call(
        functools.partial(_attn_a_kernel, s_len=s_len),
        grid=(bsz, s_len // (SUB_A * TQA)),
        in_specs=[tok, _per_batch_spec(kt), _per_batch_spec(v), _const_spec((1, A_HEADS))],
        out_specs=tok,
        out_shape=jax.ShapeDtypeStruct((bsz, s_len, 256), BF16),
        scratch_shapes=[pltpu.VMEM((2, N_CHUNKS_A, TQA, TKA), F32)],
        compiler_params=pltpu.CompilerParams(
            dimension_semantics=("arbitrary", "arbitrary"), vmem_limit_bytes=VMEM_LIMIT),
        name="attn_a",
    )(q, kt, v, sink)


def _out_proj_kernel(x_ref, oa_ref, ob_ref, oc_ref, od_ref, wo_ref, x1_ref):
    x1 = x_ref[0]
    for j, o_ref in enumerate((oa_ref, ob_ref, oc_ref, od_ref)):
        x1 = x1 + _dot(o_ref[0], wo_ref[j * 256:(j + 1) * 256, :])
    x1_ref[0] = x1


def _out_proj_call(x, oa, ob, oc, od, lw):
    bsz, s_len, _ = x.shape
    tok = lambda w: pl.BlockSpec((1, TM, w), lambda b, i: (b, i, 0))
    return pl.pallas_call(
        _out_proj_kernel,
        grid=(bsz, s_len // TM),
        in_specs=[tok(D_MODEL), tok(256), tok(256), tok(256), tok(256),
                  _const_spec((D_MODEL, D_MODEL))],
        out_specs=tok(D_MODEL),
        out_shape=jax.ShapeDtypeStruct((bsz, s_len, D_MODEL), F32),
        compiler_params=pltpu.CompilerParams(
            dimension_semantics=("arbitrary", "arbitrary"), vmem_limit_bytes=VMEM_LIMIT),
        name="out_proj",
    )(x, oa, ob, oc, od, lw["w_out"])


def _transpose8(blocks):
    sub = lax.broadcasted_iota(jnp.int32, blocks[0].shape, 0)
    for d in (4, 2, 1):
        keep = (sub & d) == 0
        nxt = list(blocks)
        for i in range(SUBLANES):
            if i & d == 0:
                lo, hi = blocks[i], blocks[i + d]
                nxt[i] = jnp.where(keep, lo, pltpu.roll(hi, d, 0))
                nxt[i + d] = jnp.where(keep, pltpu.roll(lo, SUBLANES - d, 0), hi)
        blocks = nxt
    return blocks


def _ffn_kernel(x1_ref, prev_ref, next_ref, g_ref, wu_ref, vec_ref, wd_ref, gf_ref,
                o_ref, xf_ref, xb_ref, h0_ref, h1_ref, act_ref, *, final_norm):
    first = pl.program_id(1) == 0
    last = pl.program_id(1) == pl.num_programs(1) - 1
    nv = TM // SUBLANES

    sub = lax.broadcasted_iota(jnp.int32, (SUBLANES, D_MODEL), 0)
    for jb in range(nv // SUBLANES):
        blocks = [x1_ref[0, s * nv + jb * SUBLANES:s * nv + (jb + 1) * SUBLANES, :] for s in range(SUBLANES)]
        for jj, rows in enumerate(_transpose8(blocks)):
            j = jb * SUBLANES + jj
            xf_ref[SUBLANES * (j + 1):SUBLANES * (j + 2), :] = rows
            if j == nv - 1:
                xf_ref[0:SUBLANES, :] = jnp.where(
                    sub == 0, prev_ref[0, SUBLANES - 1:SUBLANES, :], pltpu.roll(rows, 1, 0))
            if j == 0:
                xf_ref[TM + SUBLANES:TM + 2 * SUBLANES, :] = jnp.where(
                    sub == SUBLANES - 1, next_ref[0, 0:1, :], pltpu.roll(rows, SUBLANES - 1, 0))
    xb_ref[...] = _rms(xf_ref[...], g_ref[...]).astype(BF16)

    def up(c, h_ref):
        xe = xb_ref[...]
        h_ref[0] = _dot(xe, wu_ref[:, c * FF_CHUNK:(c + 1) * FF_CHUNK])
        h_ref[1] = _dot(xe, wu_ref[:, D_FF + c * FF_CHUNK:D_FF + (c + 1) * FF_CHUNK])

    sub_h = lax.broadcasted_iota(jnp.int32, (SUBLANES, FF_CHUNK), 0)

    def conv(h_ref, half, bias, w0, w1, w2, cb):
        top = h_ref[half, 0:SUBLANES, :]
        h_ref[half, 0:SUBLANES, :] = jnp.where(first & (sub_h == 0), -bias, top)
        bot = h_ref[half, TM + SUBLANES:TM + 2 * SUBLANES, :]
        h_ref[half, TM + SUBLANES:TM + 2 * SUBLANES, :] = jnp.where(last & (sub_h == SUBLANES - 1), -bias, bot)
        return (w0 * h_ref[half, 0:TM, :] + w1 * h_ref[half, SUBLANES:SUBLANES + TM, :]
                + w2 * h_ref[half, 2 * SUBLANES:2 * SUBLANES + TM, :] + (bias * (w0 + w1 + w2) + cb))

    def gate(c, h_ref):
        vec = vec_ref[c]
        ca = conv(h_ref, 0, vec[0:1], vec[2:3], vec[3:4], vec[4:5], vec[8:9])
        cb = conv(h_ref, 1, vec[1:2], vec[5:6], vec[6:7], vec[7:8], vec[9:10])
        act_ref[c] = ((ca / (1.0 + jnp.exp(-ca))) * cb).astype(BF16)

    bufs = (h0_ref, h1_ref)
    up(0, bufs[0])
    for c in range(N_FF_CHUNKS):
        if c + 1 < N_FF_CHUNKS:
            up(c + 1, bufs[(c + 1) % 2])
        gate(c, bufs[c % 2])

    for n in range(D_MODEL // FF_CHUNK):
        cols = slice(n * FF_CHUNK, (n + 1) * FF_CHUNK)
        y = _dot(act_ref[0], wd_ref[0, :, cols])
        for c in range(1, N_FF_CHUNKS):
            y = y + _dot(act_ref[c], wd_ref[c, :, cols])
        xf_ref[0:TM, cols] = y
        for jb in range(nv // SUBLANES):
            groups = [xf_ref[(jb * SUBLANES + jj) * SUBLANES:(jb * SUBLANES + jj + 1) * SUBLANES, cols]
                      for jj in range(SUBLANES)]
            for s, blk in enumerate(_transpose8(groups)):
                rows = slice(s * nv + jb * SUBLANES, s * nv + (jb + 1) * SUBLANES)
                o_ref[0, rows, cols] = x1_ref[0, rows, cols] + blk
    if final_norm:
        o_ref[0] = _rms(o_ref[0], gf_ref[...])


def _ffn_call(x1, lw, g_final, final_norm):
    bsz, s_len, _ = x1.shape
    per_tile = TM // SUBLANES
    last = s_len // SUBLANES - 1
    tok = pl.BlockSpec((1, TM, D_MODEL), lambda b, i: (b, i, 0))
    prev = pl.BlockSpec((1, SUBLANES, D_MODEL), lambda b, i: (b, jnp.maximum(i * per_tile - 1, 0), 0))
    nxt = pl.BlockSpec((1, SUBLANES, D_MODEL),
                       lambda b, i: (b, jnp.minimum((i + 1) * per_tile, last), 0))
    rows = TM + 2 * SUBLANES
    return pl.pallas_call(
        functools.partial(_ffn_kernel, final_norm=final_norm),
        grid=(bsz, s_len // TM),
        in_specs=[tok, prev, nxt, _const_spec((1, D_MODEL)),
                  _const_spec((D_MODEL, 2 * D_FF)),
                  _const_spec((N_FF_CHUNKS, 16, FF_CHUNK)),
                  _const_spec((N_FF_CHUNKS, FF_CHUNK, D_MODEL)),
                  _const_spec((1, D_MODEL))],
        out_specs=tok,
        out_shape=jax.ShapeDtypeStruct((bsz, s_len, D_MODEL), F32),
        scratch_shapes=[pltpu.VMEM((rows, D_MODEL), F32),
                        pltpu.VMEM((rows, D_MODEL), BF16),
                        pltpu.VMEM((2, rows, FF_CHUNK), F32),
                        pltpu.VMEM((2, rows, FF_CHUNK), F32),
                        pltpu.VMEM((N_FF_CHUNKS, TM, FF_CHUNK), BF16)],
        compiler_params=pltpu.CompilerParams(
            dimension_semantics=("arbitrary", "arbitrary"), vmem_limit_bytes=VMEM_LIMIT),
        name="ffn",
    )(x1, x1, x1, lw["g_ffn"], lw["w_up"], lw["ffn_vec"], lw["w_down"], g_final)


def _pack_layer(p, l):
    pieces = jnp.split(p["w_in"][l], np.cumsum(IN_SIZES)[:-1].tolist(), axis=1)
    aq, ak, av, bcq, bckv, bkr, cq, ck, cv, dq, dk, dv = pieces
    z = lambda n: jnp.zeros((D_MODEL, n), F32)
    w_in = jnp.concatenate([aq, ak, av, bcq, bckv, z(64), bkr, z(32), cq, ck, cv, dq, dk, dv], axis=1)

    wq = p["b_w_q_up"][l].reshape(B_Q_RANK, B_HEADS, B_NOPE + B_ROPE)
    wq = jnp.pad(wq, ((0, 0), (0, 0), (0, 128 - B_NOPE - B_ROPE))).reshape(B_Q_RANK, B_HEADS * 128)
    wkv = p["b_w_kv_up"][l].reshape(B_KV_RANK, B_HEADS, B_NOPE + B_V)
    wk = jnp.pad(wkv[:, :, :B_NOPE], ((0, 0), (0, 0), (0, 128 - B_NOPE))).reshape(B_KV_RANK, B_HEADS * 128)
    wv = wkv[:, :, B_NOPE:].reshape(B_KV_RANK, B_HEADS * B_V)

    halves = lambda v: (v[..., :D_FF], v[..., D_FF:])
    bua, bub = halves(p["b_up"][l])
    cwa, cwb = halves(p["conv_w"][l])
    cba, cbb = halves(p["conv_b"][l])
    rows = [bua, bub, cwa[0], cwa[1], cwa[2], cwb[0], cwb[1], cwb[2], cba, cbb]
    vec = jnp.stack(rows + [jnp.zeros_like(bua)] * (16 - len(rows)), axis=0)
    vec = vec.reshape(16, N_FF_CHUNKS, FF_CHUNK).transpose(1, 0, 2)
    row = lambda v: v.reshape(1, -1)
    return {
        "g_attn": row(p["g_attn"][l]),
        "w_in": w_in.astype(BF16),
        "a_sink": row(p["a_sink"][l]),
        "b_q_norm": row(p["b_q_norm"][l]),
        "w_q_up": wq.astype(BF16),
        "b_kv_norm": row(p["b_kv_norm"][l]),
        "w_kv_up": jnp.concatenate([wk, wv], axis=1).astype(BF16),
        "c_q_norm": row(jnp.tile(p["c_q_norm"][l], 4)),
        "c_k_norm": row(jnp.tile(p["c_k_norm"][l], 2)),
        "d_lambda": tuple(row(p[n][l]) for n in ("d_lambda_q1", "d_lambda_k1", "d_lambda_q2", "d_lambda_k2")),
        "d_sub_norm": row(jnp.tile(p["d_sub_norm"][l], 2)),
        "w_out": p["w_out"][l].astype(BF16),
        "g_ffn": row(p["g_ffn"][l]),
        "w_up": p["w_up"][l].astype(BF16),
        "ffn_vec": vec,
        "w_down": p["w_down"][l].reshape(N_FF_CHUNKS, FF_CHUNK, D_MODEL).astype(BF16),
    }


def _rope_tables(s_len):
    half = 16
    inv = ROPE_THETA ** (-jnp.arange(half, dtype=F32) * 2.0 / (2 * half))
    pos = jnp.arange(s_len, dtype=F32)
    rows = s_len // GRID_W
    row_pos = jnp.broadcast_to(jnp.arange(rows, dtype=F32)[:, None], (rows, GRID_W)).reshape(s_len)
    col_pos = jnp.broadcast_to(jnp.arange(GRID_W, dtype=F32)[None, :], (rows, GRID_W)).reshape(s_len)

    def cs(p):
        ang = p[:, None] * inv[None, :]
        return jnp.cos(ang), jnp.sin(ang)

    one = lambda n: jnp.ones((s_len, n), F32)
    zero = lambda n: jnp.zeros((s_len, n), F32)
    c, s = cs(pos)
    rope_b = jnp.stack([
        jnp.concatenate([one(64), c, c, one(32)], axis=1),
        jnp.concatenate([zero(64), -s, zero(16), zero(32)], axis=1),
        jnp.concatenate([zero(64), zero(16), s, zero(32)], axis=1)])
    cr, sr = cs(row_pos)
    cc, sc = cs(col_pos)
    z = zero(16)
    rope_c = jnp.stack([
        jnp.tile(jnp.concatenate([cr, cr, cc, cc], axis=1), (1, 2)),
        jnp.tile(jnp.concatenate([-sr, z, -sc, z], axis=1), (1, 2)),
        jnp.tile(jnp.concatenate([z, sr, z, sc], axis=1), (1, 2))])
    return rope_b, rope_c


def _trunk(x, layers, g_final):
    bsz, s_len, _ = x.shape
    nc = s_len // TK
    rope_b, rope_c = _rope_tables(s_len)
    depth = len(layers)
    for l, lw in enumerate(layers):
        qa, ka, va, qb, ktb, vb, qc, ktc, vc, qd, ktd, vd = _proj_call(x, lw, rope_b, rope_c)
        oa = _attn_a_call(qa, ka, va, lw["a_sink"])
        ob = _full_attn_call("b", qb, ktb, vb.reshape(bsz, nc, TK, 512))
        oc = _full_attn_call("c", qc, ktc, vc.reshape(bsz, 2, nc, TK, 128))
        lam_init = 0.8 - 0.6 * math.exp(-0.3 * l)
        od = _full_attn_call("d", qd, ktd, vd.reshape(bsz, nc, TK, 512),
                             extra=(*lw["d_lambda"], lw["d_sub_norm"]), lam_init=lam_init)
        x1 = _out_proj_call(x, oa, ob, oc, od, lw)
        x = _ffn_call(x1, lw, g_final, final_norm=(l == depth - 1))
    return x


def kernel(x_prompt, x_sample, g_attn, w_in, a_sink, b_q_norm, b_w_q_up, b_kv_norm, b_w_kv_up,
           c_q_norm, c_k_norm, d_lambda_q1, d_lambda_k1, d_lambda_q2, d_lambda_k2, d_sub_norm, w_out,
           g_ffn, w_up, b_up, conv_w, conv_b, w_down, g_final):
    p = dict(g_attn=g_attn, w_in=w_in, a_sink=a_sink, b_q_norm=b_q_norm, b_w_q_up=b_w_q_up,
             b_kv_norm=b_kv_norm, b_w_kv_up=b_w_kv_up, c_q_norm=c_q_norm, c_k_norm=c_k_norm,
             d_lambda_q1=d_lambda_q1, d_lambda_k1=d_lambda_k1, d_lambda_q2=d_lambda_q2,
             d_lambda_k2=d_lambda_k2, d_sub_norm=d_sub_norm, w_out=w_out, g_ffn=g_ffn, w_up=w_up,
             b_up=b_up, conv_w=conv_w, conv_b=conv_b, w_down=w_down)
    layers = [_pack_layer(p, l) for l in range(g_attn.shape[0])]
    gf = g_final.reshape(1, -1)
    return _trunk(x_prompt, layers, gf), _trunk(x_sample, layers, gf)
```
